```python
import math
import jax, jax.numpy as jnp
from jax import lax
import numpy as np

D_MODEL = 1024
BATCH = 8
SEQ = 2048
DEPTH = 1
DEC_BATCH = 16
DEC_SEQ = 4096
PAST_LEN = 128

N_DN = 4
DK_DN = 128
DV_DN = 128
DN_CONV = 5
CHUNK = 64
N_DA = 4
DH_DA = 64
Q_BLOCK = 128
ROT_DIM = DH_DA // 4
ROPE_THETA = 500000.0
D_FF = 2816
FFN_CONV = 3
PLE_DIM = 256
EPS = 1e-6

DN_QK = N_DN * DK_DN
DN_V = N_DN * DV_DN
DA_QK = N_DA * 2 * DH_DA
DA_V = N_DA * 2 * DH_DA
MIX_WIDTH = DN_V + DA_V
SPLIT_SIZES = [DN_QK, DN_QK, DN_V, DN_V, 2 * N_DN, 2 * N_DN, DA_QK, DA_QK, DA_V]
IN_COLS = int(sum(SPLIT_SIZES))
SPLIT_IDX = [int(c) for c in np.cumsum(SPLIT_SIZES)[:-1]]

kernel_name = "hybrid_bidir_deltanet_diffattn_encoder"


def rms_norm(x, g):
    xf = x.astype(jnp.float32)
    y = xf * lax.rsqrt(jnp.mean(xf * xf, axis=-1, keepdims=True) + EPS)
    return (y * g.astype(jnp.float32)).astype(x.dtype)


def l2_norm(x):
    return x * lax.rsqrt(jnp.sum(x * x, axis=-1, keepdims=True) + EPS)


def centred_dwconv(x, w):
    k = w.shape[0]
    r = k // 2
    s = x.shape[1]
    xp = jnp.pad(x, ((0, 0), (r, r), (0, 0)))
    out = xp[:, 0:s] * w[0]
    for j in range(1, k):
        out = out + xp[:, j:j + s] * w[j]
    return out


def rope_tables(s):
    inv_freq = ROPE_THETA ** (-jnp.arange(0, ROT_DIM, 2, dtype=jnp.float32) / ROT_DIM)
    ang = jnp.arange(s, dtype=jnp.float32)[:, None] * inv_freq[None, :]
    return jnp.cos(ang), jnp.sin(ang)


def partial_rope(x, cos, sin):
    c = cos[None, :, None, None, :].astype(x.dtype)
    s_ = sin[None, :, None, None, :].astype(x.dtype)
    half = ROT_DIM // 2
    x1 = x[..., :half]
    x2 = x[..., half:ROT_DIM]
    return jnp.concatenate([x1 * c - x2 * s_, x2 * c + x1 * s_, x[..., ROT_DIM:]], axis=-1)


def gated_delta_rule(q, k, v, g, beta):
    out_dtype = v.dtype
    b, t, h, dk = q.shape
    dv = v.shape[-1]
    n = t // CHUNK
    f32 = jnp.float32
    q = q.astype(f32).transpose(0, 2, 1, 3).reshape(b, h, n, CHUNK, dk)
    k = k.astype(f32).transpose(0, 2, 1, 3).reshape(b, h, n, CHUNK, dk)
    v = v.astype(f32).transpose(0, 2, 1, 3).reshape(b, h, n, CHUNK, dv)
    g = g.astype(f32).transpose(0, 2, 1).reshape(b, h, n, CHUNK)
    beta = beta.astype(f32).transpose(0, 2, 1).reshape(b, h, n, CHUNK)

    gc = jnp.cumsum(g, axis=-1)
    tril = jnp.tril(jnp.ones((CHUNK, CHUNK), dtype=bool))
    strict = jnp.tril(jnp.ones((CHUNK, CHUNK), dtype=bool), -1)
    decay = jnp.exp(jnp.where(tril, gc[..., :, None] - gc[..., None, :], -jnp.inf))

    kb = k * beta[..., None]
    lower = jnp.where(strict, jnp.einsum('bhnid,bhnjd->bhnij', kb, k) * decay, 0.0)
    rhs = jnp.concatenate([v * beta[..., None], kb * jnp.exp(gc)[..., None]], axis=-1)
    sol = lax.linalg.triangular_solve(lower, rhs, left_side=True, lower=True, unit_diagonal=True)
    u = sol[..., :dv]
    w = sol[..., dv:]

    qk = jnp.einsum('bhnid,bhnjd->bhnij', q, k) * decay
    q_dec = q * jnp.exp(gc)[..., None]
    k_dec = k * jnp.exp(gc[..., -1:] - gc)[..., None]
    g_last = jnp.exp(gc[..., -1])

    xs = tuple(jnp.moveaxis(a, 2, 0) for a in (qk, q_dec, k_dec, u, w, g_last))

    def step(state, inp):
        qk_c, qd_c, kd_c, u_c, w_c, gl_c = inp
        v_new = u_c - jnp.einsum('bhck,bhkv->bhcv', w_c, state)
        o = jnp.einsum('bhck,bhkv->bhcv', qd_c, state) + jnp.einsum('bhij,bhjv->bhiv', qk_c, v_new)
        state = state * gl_c[..., None, None] + jnp.einsum('bhck,bhcv->bhkv', kd_c, v_new)
        return state, o

    s0 = jnp.zeros((b, h, dk, dv), f32)
    _, o = lax.scan(step, s0, xs)
    o = jnp.moveaxis(o, 0, 2).reshape(b, h, t, dv).transpose(0, 2, 1, 3)
    return o.astype(out_dtype)


def diff_attention(q, k, v, lam):
    b, s, h, _, dh = q.shape
    nb = s // Q_BLOCK
    scale = dh ** -0.5
    f32 = jnp.float32
    qb = q.astype(f32).reshape(b, nb, Q_BLOCK, h, 2, dh).transpose(1, 0, 3, 4, 2, 5)
    kt = k.astype(f32).transpose(0, 2, 3, 1, 4)
    vt = v.astype(f32).transpose(0, 2, 1, 3)

    def block(qblk):
        scores = jnp.einsum('bhcqd,bhckd->bhcqk', qblk, kt) * scale
        p = jax.nn.softmax(scores, axis=-1)
        a = p[:, :, 0] - lam * p[:, :, 1]
        return jnp.einsum('bhqk,bhke->bhqe', a, vt)

    o = lax.map(block, qb)
    o = o.transpose(1, 0, 3, 2, 4).reshape(b, s, h, 2 * dh)
    return o.astype(v.dtype)


def encoder_layer(x, pe, layer_idx, ln1_g, w_in, dn_conv_w, dn_a_log, dn_dt_bias, dn_norm_g,
                  da_qk_norm_g, da_lambda, da_subln_g, w_out, ln2_g, w_up, ffn_conv_w, ffn_conv_b,
                  w_down, ple_proj, ple_norm_g, ple_gate_norm_g, w_ple_gate):
    b, s, _ = x.shape
    lam_init = 0.8 - 0.6 * math.exp(-0.3 * layer_idx)

    hn = rms_norm(x, ln1_g)
    proj = hn @ w_in
    dq, dk, dv, dz, da, db, aq, ak, av = jnp.split(proj, SPLIT_IDX, axis=-1)

    qkv = jax.nn.silu(centred_dwconv(jnp.concatenate([dq, dk, dv], axis=-1), dn_conv_w))
    dq, dk, dv = jnp.split(qkv, [DN_QK, 2 * DN_QK], axis=-1)
    dq = (l2_norm(dq.reshape(b, s, N_DN, DK_DN).astype(jnp.float32)) * (DK_DN ** -0.5))
    dk = l2_norm(dk.reshape(b, s, N_DN, DK_DN).astype(jnp.float32))
    dv = dv.reshape(b, s, N_DN, DV_DN)
    da = da.reshape(b, s, 2, N_DN).astype(jnp.float32)
    db = db.reshape(b, s, 2, N_DN).astype(jnp.float32)
    log_decay = -jnp.exp(dn_a_log.astype(jnp.float32)) * jax.nn.softplus(da + dn_dt_bias.astype(jnp.float32))
    beta = jax.nn.sigmoid(db)
    o_fwd = gated_delta_rule(dq, dk, dv, log_decay[:, :, 0], beta[:, :, 0])
    flip = lambda a: jnp.flip(a, axis=1)
    o_bwd = flip(gated_delta_rule(flip(dq), flip(dk), flip(dv), flip(log_decay[:, :, 1]), flip(beta[:, :, 1])))
    o_dn = rms_norm(o_fwd + o_bwd, dn_norm_g) * jax.nn.silu(dz.reshape(b, s, N_DN, DV_DN))

    cos, sin = rope_tables(s)
    aq = partial_rope(rms_norm(aq.reshape(b, s, N_DA, 2, DH_DA), da_qk_norm_g[0]), cos, sin)
    ak = partial_rope(rms_norm(ak.reshape(b, s, N_DA, 2, DH_DA), da_qk_norm_g[1]), cos, sin)
    av = av.reshape(b, s, N_DA, 2 * DH_DA)
    lf = da_lambda.astype(jnp.float32)
    lam = jnp.exp(jnp.sum(lf[0] * lf[1])) - jnp.exp(jnp.sum(lf[2] * lf[3])) + lam_init
    o_da = diff_attention(aq, ak, av, lam)
    o_da = rms_norm(o_da, da_subln_g) * (1.0 - lam_init)

    mix = jnp.concatenate([o_dn.reshape(b, s, DN_V), o_da.reshape(b, s, DA_V)], axis=-1)
    x = x + (mix @ w_out).astype(x.dtype)

    hn = rms_norm(x, ln2_g)
    gate, up = jnp.split(hn @ w_up, [D_FF], axis=-1)
    gate = centred_dwconv(gate, ffn_conv_w) + ffn_conv_b
    x = x + ((jax.nn.silu(gate) * up) @ w_down).astype(x.dtype)

    e = rms_norm(pe @ ple_proj, ple_norm_g)
    gt = jax.nn.sigmoid(rms_norm(x, ple_gate_norm_g) @ w_ple_gate)
    return x + (gt * e).astype(x.dtype)


def trunk(x, p, weights):
    for i in range(DEPTH):
        x = encoder_layer(x, p[i], i, *[w[i] for w in weights])
    return x


def setup_inputs(seed: int = 0) -> dict:
    key = jax.random.key(seed)
    ks = jax.random.split(key, 24)
    f32 = jnp.float32
    nrm = lambda k, shape, scale: jax.random.normal(k, shape, f32) * scale
    gain = lambda k, shape: 1.0 + 0.02 * jax.random.normal(k, shape, f32)
    dt = jnp.exp(jax.random.uniform(ks[6], (DEPTH, 2, N_DN), f32, math.log(1e-3), math.log(1e-1)))
    return {
        "x_prompt": jax.random.normal(ks[0], (BATCH, SEQ, D_MODEL), f32),
        "x_sample": jax.random.normal(ks[1], (DEC_BATCH, DEC_SEQ, D_MODEL), f32),
        "p_prompt": jax.random.normal(ks[2], (DEPTH, BATCH, SEQ, PLE_DIM), f32),
        "p_sample": jax.random.normal(ks[3], (DEPTH, DEC_BATCH, DEC_SEQ, PLE_DIM), f32),
        "ln1_g": gain(ks[4], (DEPTH, D_MODEL)),
        "w_in": nrm(ks[5], (DEPTH, D_MODEL, IN_COLS), D_MODEL ** -0.5),
        "dn_conv_w": nrm(ks[7], (DEPTH, DN_CONV, 2 * DN_QK + DN_V), DN_CONV ** -0.5),
        "dn_a_log": jnp.log(jax.random.uniform(ks[8], (DEPTH, 2, N_DN), f32, 1.0, 16.0)),
        "dn_dt_bias": dt + jnp.log(-jnp.expm1(-dt)),
        "dn_norm_g": gain(ks[9], (DEPTH, DV_DN)),
        "da_qk_norm_g": gain(ks[10], (DEPTH, 2, DH_DA)),
        "da_lambda": nrm(ks[11], (DEPTH, 4, DH_DA), 0.1),
        "da_subln_g": gain(ks[12], (DEPTH, 2 * DH_DA)),
        "w_out": nrm(ks[13], (DEPTH, MIX_WIDTH, D_MODEL), MIX_WIDTH ** -0.5),
        "ln2_g": gain(ks[14], (DEPTH, D_MODEL)),
        "w_up": nrm(ks[15], (DEPTH, D_MODEL, 2 * D_FF), D_MODEL ** -0.5),
        "ffn_conv_w": nrm(ks[16], (DEPTH, FFN_CONV, D_FF), FFN_CONV ** -0.5),
        "ffn_conv_b": nrm(ks[17], (DEPTH, D_FF), 0.02),
        "w_down": nrm(ks[18], (DEPTH, D_FF, D_MODEL), D_FF ** -0.5),
        "ple_proj": nrm(ks[19], (DEPTH, PLE_DIM, D_MODEL), PLE_DIM ** -0.5),
        "ple_norm_g": gain(ks[20], (DEPTH, D_MODEL)),
        "ple_gate_norm_g": gain(ks[21], (DEPTH, D_MODEL)),
        "w_ple_gate": nrm(ks[22], (DEPTH, D_MODEL, D_MODEL), D_MODEL ** -0.5),
    }


def reference(x_prompt, x_sample, p_prompt, p_sample, ln1_g, w_in, dn_conv_w, dn_a_log, dn_dt_bias,
              dn_norm_g, da_qk_norm_g, da_lambda, da_subln_g, w_out, ln2_g, w_up, ffn_conv_w,
              ffn_conv_b, w_down, ple_proj, ple_norm_g, ple_gate_norm_g, w_ple_gate):
    weights = (ln1_g, w_in, dn_conv_w, dn_a_log, dn_dt_bias, dn_norm_g, da_qk_norm_g, da_lambda,
               da_subln_g, w_out, ln2_g, w_up, ffn_conv_w, ffn_conv_b, w_down, ple_proj,
               ple_norm_g, ple_gate_norm_g, w_ple_gate)
    y_prompt = trunk(x_prompt, p_prompt, weights)
    y_sample = trunk(x_sample, p_sample, weights)
    return (y_prompt, y_sample)
```

```python
import functools
import math

import jax
import jax.numpy as jnp
from jax import lax
from jax.experimental import pallas as pl
from jax.experimental.pallas import tpu as pltpu

D_MODEL = 1024
N_DN = 4
DK_DN = 128
DN_CONV = 5
CHUNK = 64
N_DA = 4
DH_DA = 64
ROT_DIM = DH_DA // 4
ROPE_THETA = 500000.0
D_FF = 2816
FFN_CONV = 3
PLE_DIM = 256
EPS = 1e-6
DN_W = N_DN * DK_DN
LANES = 128
HALO = 16
VMEM_LIMIT = 56 * 1024 * 1024

F32 = jnp.float32
BF16 = jnp.bfloat16


def _dot(a, b):
    return jnp.dot(a, b, preferred_element_type=F32)


def _dot_nt(a, b):
    return lax.dot_general(a, b, (((1,), (1,)), ((), ())), preferred_element_type=F32)


def _rms(x, g):
    return x * lax.rsqrt(jnp.mean(x * x, axis=-1, keepdims=True) + EPS) * g


def _silu(x):
    return x * jax.nn.sigmoid(x)


def _cparams(sem):
    return pltpu.CompilerParams(dimension_semantics=sem, vmem_limit_bytes=VMEM_LIMIT)


def _const_spec(shape):
    nd = len(shape)
    return pl.BlockSpec(shape, lambda *_: (0,) * nd)


def _inproj_kernel(tiles_per_seq, tm,
                   xp_ref, x_ref, xn_ref, ln_ref, wdn_ref, wrest_ref, convw_ref,
                   alog_ref, dtb_ref, qkg_ref, rc_ref, rs1_ref, rs2_ref,
                   dn_ref, z_ref, gcol_ref, grow_ref, aq_ref, ak_ref, av_ref, pbuf):
    j = pl.program_id(0) % tiles_per_seq
    ln = ln_ref[...]
    xp = jnp.where(j == 0, 0.0, xp_ref[...])
    xn = jnp.where(j == tiles_per_seq - 1, 0.0, xn_ref[...])
    xx = jnp.concatenate([xp, x_ref[...], xn], axis=0)
    hn = _rms(xx, ln).astype(BF16)
    hm = hn[HALO:HALO + tm]

    pbuf[...] = _dot(hn, wdn_ref[...])
    r = DN_CONV // 2
    for c in range(3 * N_DN):
        cols = slice(c * LANES, (c + 1) * LANES)
        acc = pbuf[pl.ds(HALO - r, tm), cols] * convw_ref[0:1, cols]
        for t in range(1, DN_CONV):
            acc = acc + pbuf[pl.ds(HALO - r + t, tm), cols] * convw_ref[t:t + 1, cols]
        y = _silu(acc)
        if c < 2 * N_DN:
            y = y * lax.rsqrt(jnp.sum(y * y, axis=-1, keepdims=True) + EPS)
            if c < N_DN:
                y = y * (DK_DN ** -0.5)
        dn_ref[:, cols] = y.astype(BF16)

    wrest = wrest_ref
    z_ref[...] = _dot(hm, wrest[:, 0:DN_W]).astype(BF16)
    av_ref[...] = _dot(hm, wrest[:, 3 * DN_W:4 * DN_W]).astype(BF16)

    ab = _dot(hm, wrest[:, 4 * DN_W:4 * DN_W + LANES])
    abt = ab.T
    n8 = 2 * N_DN

    def gates(da, db, alog, dtb):
        g = -jnp.exp(alog) * jax.nn.softplus(da + dtb)
        return g, jax.nn.sigmoid(db)

    gc, bc = gates(ab[:, 0:n8], ab[:, n8:2 * n8], alog_ref[0:1, 0:n8], dtb_ref[0:1, 0:n8])
    gcol_ref[...] = jnp.concatenate([gc, bc], axis=1)
    gr, br = gates(abt[0:n8], abt[n8:2 * n8], alog_ref[:, LANES:LANES + 1][0:n8],
                   dtb_ref[:, LANES:LANES + 1][0:n8])
    grow_ref[...] = jnp.concatenate([gr, br], axis=0)

    lane = lax.broadcasted_iota(jnp.int32, (tm, LANES), 1)
    lo = lane < DH_DA
    rc, rs1, rs2 = rc_ref[...], rs1_ref[...], rs2_ref[...]
    for which, out_ref in ((0, aq_ref), (1, ak_ref)):
        base = (1 + which) * DN_W
        g = qkg_ref[which:which + 1, :]
        for h in range(N_DA):
            cols = slice(h * LANES, (h + 1) * LANES)
            a = _dot(hm, wrest[:, base + h * LANES:base + (h + 1) * LANES])
            sq = a * a
            s_lo = jnp.sum(jnp.where(lo, sq, 0.0), axis=-1, keepdims=True)
            s_hi = jnp.sum(jnp.where(lo, 0.0, sq), axis=-1, keepdims=True)
            inv = lax.rsqrt(jnp.where(lo, s_lo, s_hi) * (1.0 / DH_DA) + EPS)
            a = a * inv * g
            half = ROT_DIM // 2
            a = (a * rc + pltpu.roll(a, LANES - half, 1) * rs1
                 + pltpu.roll(a, half, 1) * rs2)
            if which == 0:
                a = a * (DH_DA ** -0.5)
            out_ref[:, cols] = a.astype(BF16)


def _inproj(x, S, prep, tm=512):
    T = x.shape[0]
    assert S % tm == 0 and T % S == 0
    nt = T // tm
    hb = tm // HALO
    nhb = T // HALO
    tps = S // tm
    kern = functools.partial(_inproj_kernel, tps, tm)
    out_shape = (
        jax.ShapeDtypeStruct((T, 3 * DN_W), BF16),
        jax.ShapeDtypeStruct((T, DN_W), BF16),
        jax.ShapeDtypeStruct((T, 16), F32),
        jax.ShapeDtypeStruct((16, T), F32),
        jax.ShapeDtypeStruct((T, DN_W), BF16),
        jax.ShapeDtypeStruct((T, DN_W), BF16),
        jax.ShapeDtypeStruct((T, DN_W), BF16),
    )
    tile = lambda w: pl.BlockSpec((tm, w), lambda i: (i, 0))
    in_specs = [
        pl.BlockSpec((HALO, D_MODEL), lambda i: (jnp.maximum(i * hb - 1, 0), 0)),
        tile(D_MODEL),
        pl.BlockSpec((HALO, D_MODEL), lambda i: (jnp.minimum((i + 1) * hb, nhb - 1), 0)),
        _const_spec((1, D_MODEL)),
        _const_spec(prep["w_dn"].shape),
        _const_spec(prep["w_rest"].shape),
        _const_spec(prep["dn_conv_w"].shape),
        _const_spec(prep["alog"].shape),
        _const_spec(prep["dtb"].shape),
        _const_spec(prep["qkg"].shape),
        pl.BlockSpec((tm, LANES), lambda i: (i % tps, 0)),
        pl.BlockSpec((tm, LANES), lambda i: (i % tps, 0)),
        pl.BlockSpec((tm, LANES), lambda i: (i % tps, 0)),
    ]
    out_specs = (tile(3 * DN_W), tile(DN_W), tile(16),
                 pl.BlockSpec((16, tm), lambda i: (0, i)),
                 tile(DN_W), tile(DN_W), tile(DN_W))
    rc, rs1, rs2 = prep["rope"][S]
    return pl.pallas_call(
        kern, grid=(nt,), in_specs=in_specs, out_specs=out_specs, out_shape=out_shape,
        scratch_shapes=[pltpu.VMEM((tm + 2 * HALO, 3 * DN_W), F32)],
        compiler_params=_cparams(("parallel",)), name="inproj",
    )(x, x, x, prep["ln1_g"], prep["w_dn"], prep["w_rest"], prep["dn_conv_w"],
      prep["alog"], prep["dtb"], prep["qkg"], rc, rs1, rs2)


DN_BT = 256


def _chunk_cumsum(x, axis, reverse):
    n = x.shape[axis]
    pos = lax.broadcasted_iota(jnp.int32, x.shape, axis) % CHUNK
    s = 1
    while s < CHUNK:
        if reverse:
            sh = pltpu.roll(x, n - s, axis)
            x = x + jnp.where(pos < CHUNK - s, sh, 0.0)
        else:
            sh = pltpu.roll(x, s, axis)
            x = x + jnp.where(pos >= s, sh, 0.0)
        s *= 2
    return x


def _dn_kernel(qkvf_ref, gcf_ref, grf_ref, qkvb_ref, gcb_ref, grb_ref,
               of_ref, ob_ref, s_ref, vn_ref):
    bt = DN_BT
    nchunk = bt // CHUNK

    @pl.when(pl.program_id(1) == 0)
    def _():
        s_ref[...] = jnp.zeros_like(s_ref)

    row = lax.broadcasted_iota(jnp.int32, (bt, bt), 0)
    col = lax.broadcasted_iota(jnp.int32, (bt, bt), 1)
    same = (row // CHUNK) == (col // CHUNK)
    eye = (row == col).astype(F32)

    def level(b):
        return ((row // (2 * b)) == (col // (2 * b))) & ((row // b) != (col // b))

    dirs = ((qkvf_ref, gcf_ref, grf_ref, of_ref), (qkvb_ref, gcb_ref, grb_ref, ob_ref))
    for d, (qkv_ref, gc_ref, gr_ref, o_ref) in enumerate(dirs):
        rev = d == 1
        gcol = gc_ref[...]
        grow = gr_ref[...]
        cum_c = _chunk_cumsum(gcol, 0, rev)
        cum_r = _chunk_cumsum(grow, 1, rev)
        if rev:
            incl = same & (row <= col)
            strict = same & (row < col)
        else:
            incl = same & (row >= col)
            strict = same & (row > col)
        for h in range(N_DN):
            ci = d * N_DN + h
            gc_c = cum_c[:, ci:ci + 1]
            gc_r = cum_r[ci:ci + 1, :]
            b_c = gcol[:, 2 * N_DN + ci:2 * N_DN + ci + 1]
            b_r = grow[2 * N_DN + ci:2 * N_DN + ci + 1, :]
            q = qkv_ref[:, h * LANES:(h + 1) * LANES]
            k = qkv_ref[:, DN_W + h * LANES:DN_W + (h + 1) * LANES]
            v = qkv_ref[:, 2 * DN_W + h * LANES:2 * DN_W + (h + 1) * LANES]
            gam = jnp.exp(jnp.where(incl, gc_c - gc_r, -jnp.inf))
            kk = _dot_nt(k, k)
            qk = _dot_nt(q, k)
            a = jnp.where(strict, b_c * kk * gam, 0.0)
            dinv = eye - jnp.where(level(1), a, 0.0)
            b = 2
            while b < CHUNK:
                aoff = jnp.where(level(b), a, 0.0).astype(BF16)
                db = dinv.astype(BF16)
                dinv = dinv - _dot(_dot(db, aoff).astype(BF16), db)
                b *= 2
            tb = (dinv * b_r).astype(BF16)
            e_c = jnp.exp(gc_c)
            kf = k.astype(F32)
            ke = (kf * e_c).astype(BF16)
            qd = (q.astype(F32) * e_c).astype(BF16)
            uw = _dot(tb, jnp.concatenate([v, ke], axis=1))
            u = uw[:, 0:LANES]
            w = uw[:, LANES:2 * LANES].astype(BF16)
            qkg = (qk * gam).astype(BF16)

            state = s_ref[d, h]
            order = range(nchunk - 1, -1, -1) if rev else range(nchunk)
            o1 = [None] * nchunk
            for c in order:
                r0 = c * CHUNK
                rows = slice(r0, r0 + CHUNK)
                last = r0 if rev else r0 + CHUNK - 1
                gl = gc_c[last:last + 1, :]
                kd = (kf[rows] * jnp.exp(gl - gc_c[rows])).astype(BF16)
                sb = state.astype(BF16)
                vnew = (u[rows] - _dot(w[rows], sb)).astype(BF16)
                o1[c] = _dot(qd[rows], sb)
                vn_ref[rows, :] = vnew
                state = state * jnp.exp(gl) + lax.dot_general(
                    kd, vnew, (((0,), (0,)), ((), ())), preferred_element_type=F32)
            s_ref[d, h] = state
            o = jnp.concatenate(o1, axis=0) + _dot(qkg, vn_ref[...])
            o_ref[:, h * LANES:(h + 1) * LANES] = o


def _deltanet(dnqkv, gcol, grow, B, S):
    T = dnqkv.shape[0]
    bt = DN_BT
    nb = S // bt
    fwd = lambda b, i: b * nb + i
    bwd = lambda b, i: b * nb + nb - 1 - i
    def specs(f):
        return [pl.BlockSpec((bt, 3 * DN_W), lambda b, i: (f(b, i), 0)),
                pl.BlockSpec((bt, 16), lambda b, i: (f(b, i), 0)),
                pl.BlockSpec((16, bt), lambda b, i: (0, f(b, i)))]
    out_spec = lambda f: pl.BlockSpec((bt, DN_W), lambda b, i: (f(b, i), 0))
    return pl.pallas_call(
        _dn_kernel, grid=(B, nb),
        in_specs=specs(fwd) + specs(bwd),
        out_specs=(out_spec(fwd), out_spec(bwd)),
        out_shape=(jax.ShapeDtypeStruct((T, DN_W), F32),) * 2,
        scratch_shapes=[pltpu.VMEM((2, N_DN, DK_DN, DK_DN), F32),
                        pltpu.VMEM((bt, DK_DN), BF16)],
        compiler_params=_cparams(("parallel", "arbitrary")), name="deltanet",
    )(dnqkv, gcol, grow, dnqkv, gcol, grow)


def _attn_kernel(S, tq, tk, lam_init,
                 lam_ref, q_ref, k_ref, v_ref, g_ref, o_ref, s_scr, p_scr):
    lf = lam_ref[...]
    lam = (jnp.exp(jnp.sum(lf[0:1] * lf[1:2], axis=-1, keepdims=True))
           - jnp.exp(jnp.sum(lf[2:3] * lf[3:4], axis=-1, keepdims=True)) + lam_init)
    q = q_ref[...]
    lane = lax.broadcasted_iota(jnp.int32, q.shape, 1)
    zero = jnp.zeros_like(q)
    qs = (jnp.where(lane < DH_DA, q, zero), jnp.where(lane < DH_DA, zero, q))
    nk = S // tk

    def p1(i, m):
        k = k_ref[pl.ds(pl.multiple_of(i * tk, tk), tk), :]
        out = []
        for c in range(2):
            s = _dot_nt(qs[c], k)
            s_scr[c, :, pl.ds(pl.multiple_of(i * tk, tk), tk)] = s
            out.append(jnp.maximum(m[c], jnp.max(s, axis=-1, keepdims=True)))
        return tuple(out)

    neg = jnp.full((tq, 1), -jnp.inf, F32)
    m = lax.fori_loop(0, nk, p1, (neg, neg))

    def p2(i, l):
        out = []
        for c in range(2):
            sl = pl.ds(pl.multiple_of(i * tk, tk), tk)
            p = jnp.exp(s_scr[c, :, sl] - m[c])
            p_scr[c, :, sl] = p.astype(BF16)
            out.append(l[c] + jnp.sum(p, axis=-1, keepdims=True))
        return tuple(out)

    z1 = jnp.zeros((tq, 1), F32)
    l = lax.fori_loop(0, nk, p2, (z1, z1))
    c1 = 1.0 / l[0]
    c2 = lam / l[1]

    def p3(i, acc):
        sl = pl.ds(pl.multiple_of(i * tk, tk), tk)
        a = p_scr[0, :, sl].astype(F32) * c1 - p_scr[1, :, sl].astype(F32) * c2
        return acc + _dot(a.astype(BF16), v_ref[sl, :])

    acc = lax.fori_loop(0, nk, p3, jnp.zeros((tq, 2 * DH_DA), F32))
    o_ref[...] = (_rms(acc, g_ref[...]) * (1.0 - lam_init)).astype(BF16)


def _attn(aq, ak, av, da_lambda, subln_g, B, S, lam_init, tq=256, tk=512):
    T = aq.shape[0]
    nq = S // tq
    kern = functools.partial(_attn_kernel, S, tq, tk, lam_init)
    kv_spec = pl.BlockSpec((S, LANES), lambda b, h, i: (b, h))
    q_spec = pl.BlockSpec((tq, LANES), lambda b, h, i: (b * nq + i, h))
    return pl.pallas_call(
        kern, grid=(B, N_DA, nq),
        in_specs=[_const_spec(da_lambda.shape), q_spec, kv_spec, kv_spec,
                  _const_spec(subln_g.shape)],
        out_specs=q_spec,
        out_shape=jax.ShapeDtypeStruct((T, DN_W), BF16),
        scratch_shapes=[pltpu.VMEM((2, tq, S), F32), pltpu.VMEM((2, tq, S), BF16)],
        compiler_params=_cparams(("parallel", "parallel", "arbitrary")), name="attn",
    )(da_lambda, aq, ak, av, subln_g)


def _outproj_kernel(of_ref, ob_ref, z_ref, oda_ref, x_ref, g_ref, wout_ref, x1_ref):
    g = g_ref[...]
    parts = []
    for h in range(N_DN):
        cols = slice(h * LANES, (h + 1) * LANES)
        o = of_ref[:, cols] + ob_ref[:, cols]
        parts.append((_rms(o, g) * _silu(z_ref[:, cols].astype(F32))).astype(BF16))
    parts.append(oda_ref[...])
    mix = jnp.concatenate(parts, axis=1)
    x1_ref[...] = x_ref[...] + _dot(mix, wout_ref[...])


def _outproj(of, ob, z, oda, x, prep, tm=512):
    T = x.shape[0]
    tile = lambda w: pl.BlockSpec((tm, w), lambda i: (i, 0))
    return pl.pallas_call(
        _outproj_kernel, grid=(T // tm,),
        in_specs=[tile(DN_W), tile(DN_W), tile(DN_W), tile(DN_W), tile(D_MODEL),
                  _const_spec(prep["dn_norm_g"].shape), _const_spec(prep["w_out"].shape)],
        out_specs=tile(D_MODEL),
        out_shape=jax.ShapeDtypeStruct((T, D_MODEL), F32),
        compiler_params=_cparams(("parallel",)), name="outproj",
    )(of, ob, z, oda, x, prep["dn_norm_g"], prep["w_out"])


FFN_FC = 256
FFN_NC = D_FF // FFN_FC


def _ffn_kernel(tiles_per_seq, tm,
                xp_ref, x_ref, xn_ref, pe_ref, ln2_ref, wg_ref, wu_ref, cw_ref, cb_ref, wd_ref,
                pproj_ref, png_ref, pgn_ref, wpg_ref, out_ref, gbuf, acc_ref):
    j = pl.program_id(0) % tiles_per_seq
    ln2 = ln2_ref[...]
    x1 = x_ref[...]
    xp = jnp.where(j == 0, 0.0, xp_ref[...])
    xn = jnp.where(j == tiles_per_seq - 1, 0.0, xn_ref[...])
    hn = _rms(jnp.concatenate([xp, x1, xn], axis=0), ln2).astype(BF16)
    hm = hn[HALO:HALO + tm]
    acc_ref[...] = jnp.zeros_like(acc_ref)
    r = FFN_CONV // 2

    def body(c, carry):
        gbuf[...] = _dot(hn, wg_ref[c])
        cw = cw_ref[c]
        gate = gbuf[pl.ds(HALO - r, tm), :] * cw[0:1]
        for t in range(1, FFN_CONV):
            gate = gate + gbuf[pl.ds(HALO - r + t, tm), :] * cw[t:t + 1]
        gate = gate + cb_ref[c]
        up = _dot(hm, wu_ref[c])
        act = (_silu(gate) * up).astype(BF16)
        acc_ref[...] += _dot(act, wd_ref[c])
        return carry

    lax.fori_loop(0, FFN_NC, body, 0)
    x2 = x1 + acc_ref[...]
    e = _rms(_dot(pe_ref[...].astype(BF16), pproj_ref[...]), png_ref[...])
    gt = jax.nn.sigmoid(_dot(_rms(x2, pgn_ref[...]).astype(BF16), wpg_ref[...]))
    out_ref[...] = x2 + gt * e


def _ffn(x1, pe, S, prep, tm=512):
    T = x1.shape[0]
    hb = tm // HALO
    nhb = T // HALO
    tps = S // tm
    kern = functools.partial(_ffn_kernel, tps, tm)
    tile = lambda w: pl.BlockSpec((tm, w), lambda i: (i, 0))
    names = ("ln2_g", "w_gate", "w_upp", "ffn_conv_w", "ffn_conv_b", "w_down",
             "ple_proj", "ple_norm_g", "ple_gate_norm_g", "w_ple_gate")
    in_specs = [
        pl.BlockSpec((HALO, D_MODEL), lambda i: (jnp.maximum(i * hb - 1, 0), 0)),
        tile(D_MODEL),
        pl.BlockSpec((HALO, D_MODEL), lambda i: (jnp.minimum((i + 1) * hb, nhb - 1), 0)),
        tile(PLE_DIM),
    ] + [pl.BlockSpec(prep[n].shape, functools.partial(lambda nd, i: (0,) * nd, prep[n].ndim),
                      pipeline_mode=pl.Buffered(1)) for n in names]
    return pl.pallas_call(
        kern, grid=(T // tm,), in_specs=in_specs, out_specs=tile(D_MODEL),
        out_shape=jax.ShapeDtypeStruct((T, D_MODEL), F32),
        scratch_shapes=[pltpu.VMEM((tm + 2 * HALO, FFN_FC), F32),
                        pltpu.VMEM((tm, D_MODEL), F32)],
        compiler_params=_cparams(("parallel",)), name="ffn",
    )(x1, x1, x1, pe, *[prep[n] for n in names])


def _rope_tables(S):
    half = ROT_DIM // 2
    inv_freq = ROPE_THETA ** (-jnp.arange(0, ROT_DIM, 2, dtype=F32) / ROT_DIM)
    ang = jnp.arange(S, dtype=F32)[:, None] * inv_freq[None, :]
    cos, sin = jnp.cos(ang), jnp.sin(ang)
    zeros = jnp.zeros((S, DH_DA - ROT_DIM), F32)
    zh = jnp.zeros((S, half), F32)
    c = jnp.concatenate([cos, cos, zeros + 1.0], axis=1)
    s1 = jnp.concatenate([-sin, zh, zeros], axis=1)
    s2 = jnp.concatenate([zh, sin, zeros], axis=1)
    rep = lambda a: jnp.concatenate([a, a], axis=1)
    return rep(c), rep(s1), rep(s2)


def _prepare(layer, seqs, ln1_g, w_in, dn_conv_w, dn_a_log, dn_dt_bias, dn_norm_g, da_qk_norm_g,
             da_lambda, da_subln_g, w_out, ln2_g, w_up, ffn_conv_w, ffn_conv_b, w_down,
             ple_proj, ple_norm_g, ple_gate_norm_g, w_ple_gate):
    row = lambda a: a[layer].reshape(1, -1).astype(F32)
    w = w_in[layer]
    o_z, o_ab, o_aq = 3 * DN_W, 4 * DN_W, 4 * DN_W + 16
    w_rest = jnp.concatenate(
        [w[:, o_z:o_ab], w[:, o_aq:o_aq + 3 * DN_W], w[:, o_ab:o_aq],
         jnp.zeros((D_MODEL, LANES - 16), w.dtype)], axis=1)
    n8 = 2 * N_DN
    flat8 = lambda a: a[layer].reshape(n8).astype(F32)
    both = lambda a: jnp.zeros((n8, 2 * LANES), F32).at[0, 0:n8].set(flat8(a)).at[:, LANES].set(flat8(a))
    chunked = lambda a: a.reshape(a.shape[0], FFN_NC, FFN_FC).transpose(1, 0, 2)
    wu = w_up[layer]
    return {
        "ln1_g": row(ln1_g),
        "w_dn": w[:, 0:3 * DN_W].astype(BF16),
        "w_rest": w_rest.astype(BF16),
        "dn_conv_w": dn_conv_w[layer].astype(F32),
        "alog": both(dn_a_log), "dtb": both(dn_dt_bias),
        "qkg": jnp.tile(da_qk_norm_g[layer].astype(F32), (1, 2)),
        "rope": {S: _rope_tables(S) for S in seqs},
        "dn_norm_g": row(dn_norm_g),
        "da_lambda": da_lambda[layer].astype(F32),
        "da_subln_g": row(da_subln_g),
        "w_out": w_out[layer].astype(BF16),
        "ln2_g": row(ln2_g),
        "w_gate": chunked(wu[:, 0:D_FF]).astype(BF16),
        "w_upp": chunked(wu[:, D_FF:]).astype(BF16),
        "ffn_conv_w": chunked(ffn_conv_w[layer].astype(F32)),
        "ffn_conv_b": ffn_conv_b[layer].astype(F32).reshape(FFN_NC, 1, FFN_FC),
        "w_down": w_down[layer].reshape(FFN_NC, FFN_FC, D_MODEL).astype(BF16),
        "ple_proj": ple_proj[layer].astype(BF16),
        "ple_norm_g": row(ple_norm_g),
        "ple_gate_norm_g": row(ple_gate_norm_g),
        "w_ple_gate": w_ple_gate[layer].astype(BF16),
    }


def _encoder_layer(x, pe, layer, prep):
    B, S, _ = x.shape
    lam_init = 0.8 - 0.6 * math.exp(-0.3 * layer)
    xf = x.reshape(B * S, D_MODEL)
    dnqkv, z, gcol, grow, aq, ak, av = _inproj(xf, S, prep)
    of, ob = _deltanet(dnqkv, gcol, grow, B, S)
    oda = _attn(aq, ak, av, prep["da_lambda"], prep["da_subln_g"], B, S, lam_init)
    x1 = _outproj(of, ob, z, oda, xf, prep)
    out = _ffn(x1, pe.reshape(B * S, PLE_DIM), S, prep)
    return out.reshape(B, S, D_MODEL)


def kernel(x_prompt, x_sample, p_prompt, p_sample, ln1_g, w_in, dn_conv_w, dn_a_log, dn_dt_bias, dn_norm_g, da_qk_norm_g, da_lambda, da_subln_g, w_out, ln2_g, w_up, ffn_conv_w, ffn_conv_b, w_down, ple_proj, ple_norm_g, ple_gate_norm_g, w_ple_gate):
    weights = (ln1_g, w_in, dn_conv_w, dn_a_log, dn_dt_bias, dn_norm_g, da_qk_norm_g, da_lambda,
               da_subln_g, w_out, ln2_g, w_up, ffn_conv_w, ffn_conv_b, w_down, ple_proj,
               ple_norm_g, ple_gate_norm_g, w_ple_gate)
    depth = w_in.shape[0]
    seqs = (x_prompt.shape[1], x_sample.shape[1])
    outs = []
    for x, p in ((x_prompt, p_prompt), (x_sample, p_sample)):
        for layer in range(depth):
            prep = _prepare(layer, seqs, *weights)
            x = _encoder_layer(x, p[layer], layer, prep)
        outs.append(x)
    return tuple(outs)
```

```python
import functools
import math

import jax
import jax.numpy as jnp
from jax import lax
from jax.experimental import pallas as pl
from jax.experimental.pallas import tpu as pltpu

D_MODEL = 1024
N_DN = 4
DK_DN = 128
DN_CONV = 5
CHUNK = 64
N_DA = 4
DH_DA = 64
ROT_DIM = DH_DA // 4
ROPE_THETA = 500000.0
D_FF = 2816
FFN_CONV = 3
PLE_DIM = 256
EPS = 1e-6
DN_W = N_DN * DK_DN
LANES = 128
HALO = 16
VMEM_LIMIT = 56 * 1024 * 1024

F32 = jnp.float32
BF16 = jnp.bfloat16
LOG2E = math.log2(math.e)


def _dot(a, b):
    return jnp.dot(a, b, preferred_element_type=F32)


def _dot_nt(a, b):
    return lax.dot_general(a, b, (((1,), (1,)), ((), ())), preferred_element_type=F32)


def _rms(x, g):
    return x * lax.rsqrt(jnp.mean(x * x, axis=-1, keepdims=True) + EPS) * g


def _silu(x):
    return x * jax.nn.sigmoid(x)


def _cparams(sem):
    return pltpu.CompilerParams(dimension_semantics=sem, vmem_limit_bytes=VMEM_LIMIT)


def _const_spec(shape):
    nd = len(shape)
    return pl.BlockSpec(shape, lambda *_: (0,) * nd)


def _inproj_kernel(tiles_per_seq, tm,
                   xp_ref, x_ref, xn_ref, ln_ref, wdn_ref, wrest_ref, convw_ref,
                   alog_ref, dtb_ref, qkg_ref, rc_ref, rs1_ref, rs2_ref,
                   dn_ref, z_ref, gcol_ref, grow_ref, aq_ref, ak_ref, av_ref, pbuf):
    j = pl.program_id(0) % tiles_per_seq
    ln = ln_ref[...]
    xp = jnp.where(j == 0, 0.0, xp_ref[...])
    xn = jnp.where(j == tiles_per_seq - 1, 0.0, xn_ref[...])
    xx = jnp.concatenate([xp, x_ref[...], xn], axis=0)
    hn = _rms(xx, ln).astype(BF16)
    hm = hn[HALO:HALO + tm]

    pbuf[...] = _dot(hn, wdn_ref[...])
    r = DN_CONV // 2
    for c in range(3 * N_DN):
        cols = slice(c * LANES, (c + 1) * LANES)
        acc = pbuf[pl.ds(HALO - r, tm), cols] * convw_ref[0:1, cols]
        for t in range(1, DN_CONV):
            acc = acc + pbuf[pl.ds(HALO - r + t, tm), cols] * convw_ref[t:t + 1, cols]
        y = _silu(acc)
        if c < 2 * N_DN:
            y = y * lax.rsqrt(jnp.sum(y * y, axis=-1, keepdims=True) + EPS)
            if c < N_DN:
                y = y * (DK_DN ** -0.5)
        dn_ref[:, cols] = y.astype(BF16)

    wrest = wrest_ref
    z_ref[...] = _dot(hm, wrest[:, 0:DN_W]).astype(BF16)
    av_ref[...] = _dot(hm, wrest[:, 3 * DN_W:4 * DN_W]).astype(BF16)

    ab = _dot(hm, wrest[:, 4 * DN_W:4 * DN_W + LANES])
    abt = ab.T
    n8 = 2 * N_DN

    def gates(da, db, alog, dtb):
        g = -jnp.exp(alog) * jax.nn.softplus(da + dtb)
        return g, jax.nn.sigmoid(db)

    gc, bc = gates(ab[:, 0:n8], ab[:, n8:2 * n8], alog_ref[0:1, 0:n8], dtb_ref[0:1, 0:n8])
    gcol_ref[...] = jnp.concatenate([gc, bc], axis=1)
    gr, br = gates(abt[0:n8], abt[n8:2 * n8], alog_ref[:, LANES:LANES + 1][0:n8],
                   dtb_ref[:, LANES:LANES + 1][0:n8])
    grow_ref[...] = jnp.concatenate([gr, br], axis=0)

    lane = lax.broadcasted_iota(jnp.int32, (tm, LANES), 1)
    lo = lane < DH_DA
    rc, rs1, rs2 = rc_ref[...], rs1_ref[...], rs2_ref[...]
    for which, out_ref in ((0, aq_ref), (1, ak_ref)):
        base = (1 + which) * DN_W
        g = qkg_ref[which:which + 1, :]
        for h in range(N_DA):
            cols = slice(h * LANES, (h + 1) * LANES)
            a = _dot(hm, wrest[:, base + h * LANES:base + (h + 1) * LANES])
            sq = a * a
            s_lo = jnp.sum(jnp.where(lo, sq, 0.0), axis=-1, keepdims=True)
            s_hi = jnp.sum(jnp.where(lo, 0.0, sq), axis=-1, keepdims=True)
            inv = lax.rsqrt(jnp.where(lo, s_lo, s_hi) * (1.0 / DH_DA) + EPS)
            a = a * inv * g
            half = ROT_DIM // 2
            a = (a * rc + pltpu.roll(a, LANES - half, 1) * rs1
                 + pltpu.roll(a, half, 1) * rs2)
            if which == 0:
                a = a * (DH_DA ** -0.5 * LOG2E)
            out_ref[:, cols] = a.astype(BF16)


def _inproj(x, S, prep, tm=512):
    T = x.shape[0]
    assert S % tm == 0 and T % S == 0
    nt = T // tm
    hb = tm // HALO
    nhb = T // HALO
    tps = S // tm
    kern = functools.partial(_inproj_kernel, tps, tm)
    out_shape = (
        jax.ShapeDtypeStruct((T, 3 * DN_W), BF16),
        jax.ShapeDtypeStruct((T, DN_W), BF16),
        jax.ShapeDtypeStruct((T, 16), F32),
        jax.ShapeDtypeStruct((16, T), F32),
        jax.ShapeDtypeStruct((T, DN_W), BF16),
        jax.ShapeDtypeStruct((T, DN_W), BF16),
        jax.ShapeDtypeStruct((T, DN_W), BF16),
    )
    tile = lambda w: pl.BlockSpec((tm, w), lambda i: (i, 0))
    in_specs = [
        pl.BlockSpec((HALO, D_MODEL), lambda i: (jnp.maximum(i * hb - 1, 0), 0)),
        tile(D_MODEL),
        pl.BlockSpec((HALO, D_MODEL), lambda i: (jnp.minimum((i + 1) * hb, nhb - 1), 0)),
        _const_spec((1, D_MODEL)),
        _const_spec(prep["w_dn"].shape),
        _const_spec(prep["w_rest"].shape),
        _const_spec(prep["dn_conv_w"].shape),
        _const_spec(prep["alog"].shape),
        _const_spec(prep["dtb"].shape),
        _const_spec(prep["qkg"].shape),
        pl.BlockSpec((tm, LANES), lambda i: (i % tps, 0)),
        pl.BlockSpec((tm, LANES), lambda i: (i % tps, 0)),
        pl.BlockSpec((tm, LANES), lambda i: (i % tps, 0)),
    ]
    out_specs = (tile(3 * DN_W), tile(DN_W), tile(16),
                 pl.BlockSpec((16, tm), lambda i: (0, i)),
                 tile(DN_W), tile(DN_W), tile(DN_W))
    rc, rs1, rs2 = prep["rope"][S]
    return pl.pallas_call(
        kern, grid=(nt,), in_specs=in_specs, out_specs=out_specs, out_shape=out_shape,
        scratch_shapes=[pltpu.VMEM((tm + 2 * HALO, 3 * DN_W), F32)],
        compiler_params=_cparams(("parallel",)), name="inproj",
    )(x, x, x, prep["ln1_g"], prep["w_dn"], prep["w_rest"], prep["dn_conv_w"],
      prep["alog"], prep["dtb"], prep["qkg"], rc, rs1, rs2)


DN_BT = 256


def _chunk_cumsum(x, axis, reverse):
    n = x.shape[axis]
    pos = lax.broadcasted_iota(jnp.int32, x.shape, axis) % CHUNK
    s = 1
    while s < CHUNK:
        if reverse:
            sh = pltpu.roll(x, n - s, axis)
            x = x + jnp.where(pos < CHUNK - s, sh, 0.0)
        else:
            sh = pltpu.roll(x, s, axis)
            x = x + jnp.where(pos >= s, sh, 0.0)
        s *= 2
    return x


def _dn_kernel(qkvf_ref, gcf_ref, grf_ref, qkvb_ref, gcb_ref, grb_ref,
               of_ref, ob_ref, s_ref, vn_ref):
    bt = DN_BT
    nchunk = bt // CHUNK

    @pl.when(pl.program_id(1) == 0)
    def _():
        s_ref[...] = jnp.zeros_like(s_ref)

    row = lax.broadcasted_iota(jnp.int32, (bt, bt), 0)
    col = lax.broadcasted_iota(jnp.int32, (bt, bt), 1)
    same = (row // CHUNK) == (col // CHUNK)
    eye = (row == col).astype(F32)

    def level(b):
        return ((row // (2 * b)) == (col // (2 * b))) & ((row // b) != (col // b))

    probs = []
    dirs = ((qkvf_ref, gcf_ref, grf_ref, of_ref), (qkvb_ref, gcb_ref, grb_ref, ob_ref))
    for d, (qkv_ref, gc_ref, gr_ref, o_ref) in enumerate(dirs):
        rev = d == 1
        gcol = gc_ref[...]
        grow = gr_ref[...]
        cum_c = _chunk_cumsum(gcol, 0, rev)
        cum_r = _chunk_cumsum(grow, 1, rev)
        if rev:
            incl = same & (row <= col)
            strict = same & (row < col)
        else:
            incl = same & (row >= col)
            strict = same & (row > col)
        for h in range(N_DN):
            ci = d * N_DN + h
            probs.append(dict(
                d=d, h=h, rev=rev, o_ref=o_ref, incl=incl, strict=strict,
                gc_c=cum_c[:, ci:ci + 1], gc_r=cum_r[ci:ci + 1, :],
                b_c=gcol[:, 2 * N_DN + ci:2 * N_DN + ci + 1],
                b_r=grow[2 * N_DN + ci:2 * N_DN + ci + 1, :],
                q=qkv_ref[:, h * LANES:(h + 1) * LANES],
                k=qkv_ref[:, DN_W + h * LANES:DN_W + (h + 1) * LANES],
                v=qkv_ref[:, 2 * DN_W + h * LANES:2 * DN_W + (h + 1) * LANES]))

    for p in probs:
        gam = jnp.exp(jnp.where(p["incl"], p["gc_c"] - p["gc_r"], -jnp.inf))
        kk = _dot_nt(p["k"], p["k"])
        qk = _dot_nt(p["q"], p["k"])
        p["a"] = jnp.where(p["strict"], p["b_c"] * kk * gam, 0.0)
        p["qkg"] = (qk * gam).astype(BF16)
        p["dinv"] = eye - jnp.where(level(1), p["a"], 0.0)

    b = 2
    while b < CHUNK:
        lm = level(b)
        for p in probs:
            p["db"] = p["dinv"].astype(BF16)
            p["x"] = _dot(p["db"], jnp.where(lm, p["a"], 0.0).astype(BF16)).astype(BF16)
        for p in probs:
            p["dinv"] = p["dinv"] - _dot(p["x"], p["db"])
        b *= 2

    for p in probs:
        e_c = jnp.exp(p["gc_c"])
        kf = p["k"].astype(F32)
        ke = (kf * e_c).astype(BF16)
        qd = (p["q"].astype(F32) * e_c).astype(BF16)
        tb = (p["dinv"] * p["b_r"]).astype(BF16)
        uw = _dot(tb, jnp.concatenate([p["v"], ke], axis=1))
        p["u"] = uw[:, 0:LANES]
        w = uw[:, LANES:2 * LANES].astype(BF16)
        p["wq"], p["kd"], p["egl"] = [], [], []
        for c in range(nchunk):
            rows = slice(c * CHUNK, (c + 1) * CHUNK)
            last = c * CHUNK if p["rev"] else (c + 1) * CHUNK - 1
            gl = p["gc_c"][last:last + 1, :]
            p["wq"].append(jnp.concatenate([w[rows], qd[rows]], axis=0))
            p["kd"].append((kf[rows] * jnp.exp(gl - p["gc_c"][rows])).astype(BF16))
            p["egl"].append(jnp.exp(gl))
        p["state"] = s_ref[p["d"], p["h"]]
        p["o1"] = [None] * nchunk

    for step in range(nchunk):
        for i, p in enumerate(probs):
            c = nchunk - 1 - step if p["rev"] else step
            rows = slice(c * CHUNK, (c + 1) * CHUNK)
            pq = _dot(p["wq"][c], p["state"].astype(BF16))
            vnew = (p["u"][rows] - pq[0:CHUNK]).astype(BF16)
            p["o1"][c] = pq[CHUNK:2 * CHUNK]
            vn_ref[i, rows, :] = vnew
            p["state"] = p["state"] * p["egl"][c] + lax.dot_general(
                p["kd"][c], vnew, (((0,), (0,)), ((), ())), preferred_element_type=F32)

    for i, p in enumerate(probs):
        s_ref[p["d"], p["h"]] = p["state"]
        o = jnp.concatenate(p["o1"], axis=0) + _dot(p["qkg"], vn_ref[i])
        p["o_ref"][:, p["h"] * LANES:(p["h"] + 1) * LANES] = o


def _deltanet(dnqkv, gcol, grow, B, S):
    T = dnqkv.shape[0]
    bt = DN_BT
    nb = S // bt
    fwd = lambda b, i: b * nb + i
    bwd = lambda b, i: b * nb + nb - 1 - i
    def specs(f):
        return [pl.BlockSpec((bt, 3 * DN_W), lambda b, i: (f(b, i), 0)),
                pl.BlockSpec((bt, 16), lambda b, i: (f(b, i), 0)),
                pl.BlockSpec((16, bt), lambda b, i: (0, f(b, i)))]
    out_spec = lambda f: pl.BlockSpec((bt, DN_W), lambda b, i: (f(b, i), 0))
    return pl.pallas_call(
        _dn_kernel, grid=(B, nb),
        in_specs=specs(fwd) + specs(bwd),
        out_specs=(out_spec(fwd), out_spec(bwd)),
        out_shape=(jax.ShapeDtypeStruct((T, DN_W), F32),) * 2,
        scratch_shapes=[pltpu.VMEM((2, N_DN, DK_DN, DK_DN), F32),
                        pltpu.VMEM((2 * N_DN, bt, DK_DN), BF16)],
        compiler_params=_cparams(("parallel", "arbitrary")), name="deltanet",
    )(dnqkv, gcol, grow, dnqkv, gcol, grow)


ATTN_TKB = 1024


def _attn_kernel(S, tq, tk, lam_init,
                 lam_ref, q_ref, k_ref, v_ref, g_ref, o_ref, s_scr, vaug):
    @pl.when(pl.program_id(2) == 0)
    def _():
        lane = lax.broadcasted_iota(jnp.int32, (S, LANES), 1)
        vaug[:, 0:LANES] = v_ref[...]
        vaug[:, LANES:2 * LANES] = jnp.where(lane == 0, 1.0, 0.0).astype(BF16)

    lf = lam_ref[...]
    lam = (jnp.exp(jnp.sum(lf[0:1] * lf[1:2], axis=-1, keepdims=True))
           - jnp.exp(jnp.sum(lf[2:3] * lf[3:4], axis=-1, keepdims=True)) + lam_init)
    q = q_ref[...]
    lane = lax.broadcasted_iota(jnp.int32, q.shape, 1)
    zero = jnp.zeros_like(q)
    qs = (jnp.where(lane < DH_DA, q, zero), jnp.where(lane < DH_DA, zero, q))

    m = []
    for c in range(2):
        mp = jnp.full((tq, LANES), -jnp.inf, F32)
        for i in range(S // tk):
            s = _dot_nt(qs[c], k_ref[i * tk:(i + 1) * tk, :])
            s_scr[c, :, i * tk:(i + 1) * tk] = s
            for j in range(tk // LANES):
                mp = jnp.maximum(mp, s[:, j * LANES:(j + 1) * LANES])
        m.append(jnp.max(mp, axis=-1, keepdims=True))

    outs = []
    for c in range(2):
        acc = jnp.zeros((tq, 2 * LANES), F32)
        for i in range(S // ATTN_TKB):
            sl = slice(i * ATTN_TKB, (i + 1) * ATTN_TKB)
            p = jnp.exp2(s_scr[c, :, sl] - m[c]).astype(BF16)
            acc = acc + _dot(p, vaug[sl, :])
        outs.append(acc)
    o1 = outs[0][:, 0:LANES] / outs[0][:, LANES:LANES + 1]
    o2 = outs[1][:, 0:LANES] / outs[1][:, LANES:LANES + 1]
    o_ref[...] = (_rms(o1 - lam * o2, g_ref[...]) * (1.0 - lam_init)).astype(BF16)


def _attn(aq, ak, av, da_lambda, subln_g, B, S, lam_init, tq=256, tk=512):
    T = aq.shape[0]
    nq = S // tq
    kern = functools.partial(_attn_kernel, S, tq, tk, lam_init)
    kv_spec = pl.BlockSpec((S, LANES), lambda b, h, i: (b, h))
    q_spec = pl.BlockSpec((tq, LANES), lambda b, h, i: (b * nq + i, h))
    return pl.pallas_call(
        kern, grid=(B, N_DA, nq),
        in_specs=[_const_spec(da_lambda.shape), q_spec, kv_spec, kv_spec,
                  _const_spec(subln_g.shape)],
        out_specs=q_spec,
        out_shape=jax.ShapeDtypeStruct((T, DN_W), BF16),
        scratch_shapes=[pltpu.VMEM((2, tq, S), F32), pltpu.VMEM((S, 2 * LANES), BF16)],
        compiler_params=_cparams(("parallel", "parallel", "arbitrary")), name="attn",
    )(da_lambda, aq, ak, av, subln_g)


def _outproj_kernel(of_ref, ob_ref, z_ref, oda_ref, x_ref, g_ref, wout_ref, x1_ref):
    g = g_ref[...]
    parts = []
    for h in range(N_DN):
        cols = slice(h * LANES, (h + 1) * LANES)
        o = of_ref[:, cols] + ob_ref[:, cols]
        parts.append((_rms(o, g) * _silu(z_ref[:, cols].astype(F32))).astype(BF16))
    parts.append(oda_ref[...])
    mix = jnp.concatenate(parts, axis=1)
    x1_ref[...] = x_ref[...] + _dot(mix, wout_ref[...])


def _outproj(of, ob, z, oda, x, prep, tm=512):
    T = x.shape[0]
    tile = lambda w: pl.BlockSpec((tm, w), lambda i: (i, 0))
    return pl.pallas_call(
        _outproj_kernel, grid=(T // tm,),
        in_specs=[tile(DN_W), tile(DN_W), tile(DN_W), tile(DN_W), tile(D_MODEL),
                  _const_spec(prep["dn_norm_g"].shape), _const_spec(prep["w_out"].shape)],
        out_specs=tile(D_MODEL),
        out_shape=jax.ShapeDtypeStruct((T, D_MODEL), F32),
        compiler_params=_cparams(("parallel",)), name="outproj",
    )(of, ob, z, oda, x, prep["dn_norm_g"], prep["w_out"])


FFN_FC = 256
FFN_NC = D_FF // FFN_FC


def _ffn_kernel(tiles_per_seq, tm,
                xp_ref, x_ref, xn_ref, pe_ref, ln2_ref, wg_ref, wu_ref, cw_ref, cb_ref, wd_ref,
                pproj_ref, png_ref, pgn_ref, wpg_ref, out_ref, gbuf, acc_ref):
    j = pl.program_id(0) % tiles_per_seq
    ln2 = ln2_ref[...]
    x1 = x_ref[...]
    xp = jnp.where(j == 0, 0.0, xp_ref[...])
    xn = jnp.where(j == tiles_per_seq - 1, 0.0, xn_ref[...])
    hn = _rms(jnp.concatenate([xp, x1, xn], axis=0), ln2).astype(BF16)
    hm = hn[HALO:HALO + tm]
    acc_ref[...] = jnp.zeros_like(acc_ref)
    r = FFN_CONV // 2

    def body(c, carry):
        gbuf[...] = _dot(hn, wg_ref[c])
        cw = cw_ref[c]
        gate = gbuf[pl.ds(HALO - r, tm), :] * cw[0:1]
        for t in range(1, FFN_CONV):
            gate = gate + gbuf[pl.ds(HALO - r + t, tm), :] * cw[t:t + 1]
        gate = gate + cb_ref[c]
        up = _dot(hm, wu_ref[c])
        act = (_silu(gate) * up).astype(BF16)
        acc_ref[...] += _dot(act, wd_ref[c])
        return carry

    lax.fori_loop(0, FFN_NC, body, 0)
    x2 = x1 + acc_ref[...]
    e = _rms(_dot(pe_ref[...].astype(BF16), pproj_ref[...]), png_ref[...])
    gt = jax.nn.sigmoid(_dot(_rms(x2, pgn_ref[...]).astype(BF16), wpg_ref[...]))
    out_ref[...] = x2 + gt * e


def _ffn(x1, pe, S, prep, tm=512):
    T = x1.shape[0]
    hb = tm // HALO
    nhb = T // HALO
    tps = S // tm
    kern = functools.partial(_ffn_kernel, tps, tm)
    tile = lambda w: pl.BlockSpec((tm, w), lambda i: (i, 0))
    names = ("ln2_g", "w_gate", "w_upp", "ffn_conv_w", "ffn_conv_b", "w_down",
             "ple_proj", "ple_norm_g", "ple_gate_norm_g", "w_ple_gate")
    in_specs = [
        pl.BlockSpec((HALO, D_MODEL), lambda i: (jnp.maximum(i * hb - 1, 0), 0)),
        tile(D_MODEL),
        pl.BlockSpec((HALO, D_MODEL), lambda i: (jnp.minimum((i + 1) * hb, nhb - 1), 0)),
        tile(PLE_DIM),
    ] + [pl.BlockSpec(prep[n].shape, functools.partial(lambda nd, i: (0,) * nd, prep[n].ndim),
                      pipeline_mode=pl.Buffered(1)) for n in names]
    return pl.pallas_call(
        kern, grid=(T // tm,), in_specs=in_specs, out_specs=tile(D_MODEL),
        out_shape=jax.ShapeDtypeStruct((T, D_MODEL), F32),
        scratch_shapes=[pltpu.VMEM((tm + 2 * HALO, FFN_FC), F32),
                        pltpu.VMEM((tm, D_MODEL), F32)],
        compiler_params=_cparams(("parallel",)), name="ffn",
    )(x1, x1, x1, pe, *[prep[n] for n in names])


def _rope_tables(S):
    half = ROT_DIM // 2
    inv_freq = ROPE_THETA ** (-jnp.arange(0, ROT_DIM, 2, dtype=F32) / ROT_DIM)
    ang = jnp.arange(S, dtype=F32)[:, None] * inv_freq[None, :]
    cos, sin = jnp.cos(ang), jnp.sin(ang)
    zeros = jnp.zeros((S, DH_DA - ROT_DIM), F32)
    zh = jnp.zeros((S, half), F32)
    c = jnp.concatenate([cos, cos, zeros + 1.0], axis=1)
    s1 = jnp.concatenate([-sin, zh, zeros], axis=1)
    s2 = jnp.concatenate([zh, sin, zeros], axis=1)
    rep = lambda a: jnp.concatenate([a, a], axis=1)
    return rep(c), rep(s1), rep(s2)


def _prepare(layer, seqs, ln1_g, w_in, dn_conv_w, dn_a_log, dn_dt_bias, dn_norm_g, da_qk_norm_g,
             da_lambda, da_subln_g, w_out, ln2_g, w_up, ffn_conv_w, ffn_conv_b, w_down,
             ple_proj, ple_norm_g, ple_gate_norm_g, w_ple_gate):
    row = lambda a: a[layer].reshape(1, -1).astype(F32)
    w = w_in[layer]
    o_z, o_ab, o_aq = 3 * DN_W, 4 * DN_W, 4 * DN_W + 16
    w_rest = jnp.concatenate(
        [w[:, o_z:o_ab], w[:, o_aq:o_aq + 3 * DN_W], w[:, o_ab:o_aq],
         jnp.zeros((D_MODEL, LANES - 16), w.dtype)], axis=1)
    n8 = 2 * N_DN
    flat8 = lambda a: a[layer].reshape(n8).astype(F32)
    both = lambda a: jnp.zeros((n8, 2 * LANES), F32).at[0, 0:n8].set(flat8(a)).at[:, LANES].set(flat8(a))
    chunked = lambda a: a.reshape(a.shape[0], FFN_NC, FFN_FC).transpose(1, 0, 2)
    wu = w_up[layer]
    return {
        "ln1_g": row(ln1_g),
        "w_dn": w[:, 0:3 * DN_W].astype(BF16),
        "w_rest": w_rest.astype(BF16),
        "dn_conv_w": dn_conv_w[layer].astype(F32),
        "alog": both(dn_a_log), "dtb": both(dn_dt_bias),
        "qkg": jnp.tile(da_qk_norm_g[layer].astype(F32), (1, 2)),
        "rope": {S: _rope_tables(S) for S in seqs},
        "dn_norm_g": row(dn_norm_g),
        "da_lambda": da_lambda[layer].astype(F32),
        "da_subln_g": row(da_subln_g),
        "w_out": w_out[layer].astype(BF16),
        "ln2_g": row(ln2_g),
        "w_gate": chunked(wu[:, 0:D_FF]).astype(BF16),
        "w_upp": chunked(wu[:, D_FF:]).astype(BF16),
        "ffn_conv_w": chunked(ffn_conv_w[layer].astype(F32)),
        "ffn_conv_b": ffn_conv_b[layer].astype(F32).reshape(FFN_NC, 1, FFN_FC),
        "w_down": w_down[layer].reshape(FFN_NC, FFN_FC, D_MODEL).astype(BF16),
        "ple_proj": ple_proj[layer].astype(BF16),
        "ple_norm_g": row(ple_norm_g),
        "ple_gate_norm_g": row(ple_gate_norm_g),
        "w_ple_gate": w_ple_gate[layer].astype(BF16),
    }


def _encoder_layer(x, pe, layer, prep):
    B, S, _ = x.shape
    lam_init = 0.8 - 0.6 * math.exp(-0.3 * layer)
    xf = x.reshape(B * S, D_MODEL)
    dnqkv, z, gcol, grow, aq, ak, av = _inproj(xf, S, prep)
    of, ob = _deltanet(dnqkv, gcol, grow, B, S)
    oda = _attn(aq, ak, av, prep["da_lambda"], prep["da_subln_g"], B, S, lam_init)
    x1 = _outproj(of, ob, z, oda, xf, prep)
    out = _ffn(x1, pe.reshape(B * S, PLE_DIM), S, prep)
    return out.reshape(B, S, D_MODEL)


def kernel(x_prompt, x_sample, p_prompt, p_sample, ln1_g, w_in, dn_conv_w, dn_a_log, dn_dt_bias, dn_norm_g, da_qk_norm_g, da_lambda, da_subln_g, w_out, ln2_g, w_up, ffn_conv_w, ffn_conv_b, w_down, ple_proj, ple_norm_g, ple_gate_norm_g, w_ple_gate):
    weights = (ln1_g, w_in, dn_conv_w, dn_a_log, dn_dt_bias, dn_norm_g, da_qk_norm_g, da_lambda,
               da_subln_g, w_out, ln2_g, w_up, ffn_conv_w, ffn_conv_b, w_down, ple_proj,
               ple_norm_g, ple_gate_norm_g, w_ple_gate)
    depth = w_in.shape[0]
    seqs = (x_prompt.shape[1], x_sample.shape[1])
    outs = []
    for x, p in ((x_prompt, p_prompt), (x_sample, p_sample)):
        for layer in range(depth):
            prep = _prepare(layer, seqs, *weights)
            x = _encoder_layer(x, p[layer], layer, prep)
        outs.append(x)
    return tuple(outs)
```

```python
import functools
import math

import jax
import jax.numpy as jnp
from jax import lax
from jax.experimental import pallas as pl
from jax.experimental.pallas import tpu as pltpu

D_MODEL = 1024
N_DN = 4
DK_DN = 128
DN_CONV = 5
CHUNK = 64
N_DA = 4
DH_DA = 64
ROT_DIM = DH_DA // 4
ROPE_THETA = 500000.0
D_FF = 2816
FFN_CONV = 3
PLE_DIM = 256
EPS = 1e-6
DN_W = N_DN * DK_DN
LANES = 128
HALO = 16
VMEM_LIMIT = 56 * 1024 * 1024

F32 = jnp.float32
BF16 = jnp.bfloat16
LOG2E = math.log2(math.e)


def _dot(a, b):
    return jnp.dot(a, b, preferred_element_type=F32)


def _dot_nt(a, b):
    return lax.dot_general(a, b, (((1,), (1,)), ((), ())), preferred_element_type=F32)


def _rms(x, g):
    return x * lax.rsqrt(jnp.mean(x * x, axis=-1, keepdims=True) + EPS) * g


def _silu(x):
    return x * jax.nn.sigmoid(x)


def _cparams(sem):
    return pltpu.CompilerParams(dimension_semantics=sem, vmem_limit_bytes=VMEM_LIMIT)


def _const_spec(shape):
    nd = len(shape)
    return pl.BlockSpec(shape, lambda *_: (0,) * nd)


def _row_groups(tm):
    groups = ((0, 36), (288, 28))
    assert tm == sum(8 * s for _, s in groups)
    return groups


def _inproj_kernel(tiles_per_seq, tm,
                   xp_ref, x_ref, xn_ref, ln_ref, wdn_ref, wrest_ref, convw_ref,
                   alog_ref, dtb_ref, qkg_ref, rc_ref, rs1_ref, rs2_ref,
                   dn_ref, z_ref, gcol_ref, grow_ref, aq_ref, ak_ref, av_ref, pbuf, ybuf):
    j = pl.program_id(0) % tiles_per_seq
    ln = ln_ref[...]
    xp = jnp.where(j == 0, 0.0, xp_ref[...])
    xn = jnp.where(j == tiles_per_seq - 1, 0.0, xn_ref[...])
    xx = jnp.concatenate([xp, x_ref[...], xn], axis=0)
    hn = _rms(xx, ln).astype(BF16)
    hm = hn[HALO:HALO + tm]

    wrest = wrest_ref
    lane = lax.broadcasted_iota(jnp.int32, (tm, LANES), 1)
    lo = lane < DH_DA
    rc, rs1, rs2 = rc_ref[...], rs1_ref[...], rs2_ref[...]
    half = ROT_DIM // 2
    for which, out_ref in ((0, aq_ref), (1, ak_ref)):
        base = (1 + which) * DN_W
        g = qkg_ref[which:which + 1, :]
        for h2 in range(N_DA // 2):
            pair = _dot(hm, wrest[:, base + h2 * 2 * LANES:base + (h2 + 1) * 2 * LANES])
            for h in (2 * h2, 2 * h2 + 1):
                a = pair[:, (h % 2) * LANES:(h % 2 + 1) * LANES]
                sq = a * a
                s_lo = jnp.sum(jnp.where(lo, sq, 0.0), axis=-1, keepdims=True)
                s_hi = jnp.sum(jnp.where(lo, 0.0, sq), axis=-1, keepdims=True)
                inv = lax.rsqrt(jnp.where(lo, s_lo, s_hi) * (1.0 / DH_DA) + EPS)
                a = a * inv * g
                a = (a * rc + pltpu.roll(a, LANES - half, 1) * rs1
                     + pltpu.roll(a, half, 1) * rs2)
                if which == 0:
                    a = a * (DH_DA ** -0.5 * LOG2E)
                out_ref[:, h * LANES:(h + 1) * LANES] = a.astype(BF16)

    ab = _dot(hm, wrest[:, 4 * DN_W:4 * DN_W + LANES])
    abt = ab.T
    n8 = 2 * N_DN

    def gates(da, db, alog, dtb):
        g = -jnp.exp(alog) * jax.nn.softplus(da + dtb)
        return g, jax.nn.sigmoid(db)

    gc, bc = gates(ab[:, 0:n8], ab[:, n8:2 * n8], alog_ref[0:1, 0:n8], dtb_ref[0:1, 0:n8])
    gcol_ref[...] = jnp.concatenate([gc, bc], axis=1)
    gr, br = gates(abt[0:n8], abt[n8:2 * n8], alog_ref[:, LANES:LANES + 1][0:n8],
                   dtb_ref[:, LANES:LANES + 1][0:n8])
    grow_ref[...] = jnp.concatenate([gr, br], axis=0)

    for c2 in range(3 * N_DN // 2):
        res = _dot(hn, wdn_ref[:, c2 * 2 * LANES:(c2 + 1) * 2 * LANES])
        pbuf[2 * c2] = res[:, 0:LANES]
        pbuf[2 * c2 + 1] = res[:, LANES:2 * LANES]
    r = DN_CONV // 2
    for c in range(3 * N_DN):
        cols = slice(c * LANES, (c + 1) * LANES)
        taps = [convw_ref[t:t + 1, cols] for t in range(DN_CONV)]
        pieces = []
        for r0, s in _row_groups(tm):
            for b in range(s):
                acc = pbuf[c, pl.ds(HALO - r + r0 + b, 8, stride=s), :] * taps[0]
                for t in range(1, DN_CONV):
                    acc = acc + pbuf[c, pl.ds(HALO - r + t + r0 + b, 8, stride=s), :] * taps[t]
                pieces.append(acc)
        y = _silu(jnp.concatenate(pieces, axis=0))
        if c < 2 * N_DN:
            y = y * lax.rsqrt(jnp.sum(y * y, axis=-1, keepdims=True) + EPS)
            if c < N_DN:
                y = y * (DK_DN ** -0.5)
        n = 0
        for r0, s in _row_groups(tm):
            for b in range(s):
                ybuf[c, pl.ds(r0 + b, 8, stride=s), :] = y[8 * n:8 * n + 8]
                n += 1
        dn_ref[:, cols] = ybuf[c].astype(BF16)

    z_ref[...] = _dot(hm, wrest[:, 0:DN_W]).astype(BF16)
    av_ref[...] = _dot(hm, wrest[:, 3 * DN_W:4 * DN_W]).astype(BF16)


def _inproj(x, S, prep, tm=512):
    T = x.shape[0]
    assert S % tm == 0 and T % S == 0
    nt = T // tm
    hb = tm // HALO
    nhb = T // HALO
    tps = S // tm
    kern = functools.partial(_inproj_kernel, tps, tm)
    out_shape = (
        jax.ShapeDtypeStruct((T, 3 * DN_W), BF16),
        jax.ShapeDtypeStruct((T, DN_W), BF16),
        jax.ShapeDtypeStruct((T, 16), F32),
        jax.ShapeDtypeStruct((16, T), F32),
        jax.ShapeDtypeStruct((T, DN_W), BF16),
        jax.ShapeDtypeStruct((T, DN_W), BF16),
        jax.ShapeDtypeStruct((T, DN_W), BF16),
    )
    tile = lambda w: pl.BlockSpec((tm, w), lambda i: (i, 0))
    in_specs = [
        pl.BlockSpec((HALO, D_MODEL), lambda i: (jnp.maximum(i * hb - 1, 0), 0)),
        tile(D_MODEL),
        pl.BlockSpec((HALO, D_MODEL), lambda i: (jnp.minimum((i + 1) * hb, nhb - 1), 0)),
        _const_spec((1, D_MODEL)),
        _const_spec(prep["w_dn"].shape),
        _const_spec(prep["w_rest"].shape),
        _const_spec(prep["dn_conv_w"].shape),
        _const_spec(prep["alog"].shape),
        _const_spec(prep["dtb"].shape),
        _const_spec(prep["qkg"].shape),
        pl.BlockSpec((tm, LANES), lambda i: (i % tps, 0)),
        pl.BlockSpec((tm, LANES), lambda i: (i % tps, 0)),
        pl.BlockSpec((tm, LANES), lambda i: (i % tps, 0)),
    ]
    out_specs = (tile(3 * DN_W), tile(DN_W), tile(16),
                 pl.BlockSpec((16, tm), lambda i: (0, i)),
                 tile(DN_W), tile(DN_W), tile(DN_W))
    rc, rs1, rs2 = prep["rope"][S]
    return pl.pallas_call(
        kern, grid=(nt,), in_specs=in_specs, out_specs=out_specs, out_shape=out_shape,
        scratch_shapes=[pltpu.VMEM((3 * N_DN, tm + 2 * HALO, LANES), F32),
                        pltpu.VMEM((3 * N_DN, tm, LANES), F32)],
        compiler_params=_cparams(("parallel",)), name="inproj",
    )(x, x, x, prep["ln1_g"], prep["w_dn"], prep["w_rest"], prep["dn_conv_w"],
      prep["alog"], prep["dtb"], prep["qkg"], rc, rs1, rs2)


DN_BT = 256


def _chunk_cumsum(x, axis, reverse):
    n = x.shape[axis]
    pos = lax.broadcasted_iota(jnp.int32, x.shape, axis) % CHUNK
    s = 1
    while s < CHUNK:
        if reverse:
            sh = pltpu.roll(x, n - s, axis)
            x = x + jnp.where(pos < CHUNK - s, sh, 0.0)
        else:
            sh = pltpu.roll(x, s, axis)
            x = x + jnp.where(pos >= s, sh, 0.0)
        s *= 2
    return x


def _dn_kernel(qkvf_ref, gcf_ref, grf_ref, qkvb_ref, gcb_ref, grb_ref,
               o_ref, s_ref, vn_ref):
    bt = DN_BT
    nchunk = bt // CHUNK
    step = pl.program_id(1)
    nb = pl.num_programs(1)

    @pl.when(step == 0)
    def _():
        s_ref[...] = jnp.zeros_like(s_ref)
        o_ref[...] = jnp.zeros_like(o_ref)

    row = lax.broadcasted_iota(jnp.int32, (bt, bt), 0)
    col = lax.broadcasted_iota(jnp.int32, (bt, bt), 1)
    same = (row // CHUNK) == (col // CHUNK)
    eye = (row == col).astype(F32)

    def level(b):
        return ((row // (2 * b)) == (col // (2 * b))) & ((row // b) != (col // b))

    probs = []
    dirs = ((qkvf_ref, gcf_ref, grf_ref, step), (qkvb_ref, gcb_ref, grb_ref, nb - 1 - step))
    for d, (qkv_ref, gc_ref, gr_ref, blk) in enumerate(dirs):
        rev = d == 1
        gcol = gc_ref[...]
        grow = gr_ref[...]
        cum_c = _chunk_cumsum(gcol, 0, rev)
        cum_r = _chunk_cumsum(grow, 1, rev)
        if rev:
            incl = same & (row <= col)
            strict = same & (row < col)
        else:
            incl = same & (row >= col)
            strict = same & (row > col)
        for h in range(N_DN):
            ci = d * N_DN + h
            probs.append(dict(
                d=d, h=h, rev=rev, blk=blk, incl=incl, strict=strict,
                gc_c=cum_c[:, ci:ci + 1], gc_r=cum_r[ci:ci + 1, :],
                b_c=gcol[:, 2 * N_DN + ci:2 * N_DN + ci + 1],
                b_r=grow[2 * N_DN + ci:2 * N_DN + ci + 1, :],
                q=qkv_ref[:, h * LANES:(h + 1) * LANES],
                k=qkv_ref[:, DN_W + h * LANES:DN_W + (h + 1) * LANES],
                v=qkv_ref[:, 2 * DN_W + h * LANES:2 * DN_W + (h + 1) * LANES]))

    for p in probs:
        gam = jnp.exp(jnp.where(p["incl"], p["gc_c"] - p["gc_r"], -jnp.inf))
        kk = _dot_nt(p["k"], p["k"])
        qk = _dot_nt(p["q"], p["k"])
        p["a"] = jnp.where(p["strict"], p["b_c"] * kk * gam, 0.0)
        p["qkg"] = (qk * gam).astype(BF16)
        p["dinv"] = eye - jnp.where(level(1), p["a"], 0.0)

    b = 2
    while b < CHUNK:
        lm = level(b)
        for p in probs:
            p["db"] = p["dinv"].astype(BF16)
            p["x"] = _dot(p["db"], jnp.where(lm, p["a"], 0.0).astype(BF16)).astype(BF16)
        for p in probs:
            p["dinv"] = p["dinv"] - _dot(p["x"], p["db"])
        b *= 2

    for p in probs:
        e_c = jnp.exp(p["gc_c"])
        kf = p["k"].astype(F32)
        ke = (kf * e_c).astype(BF16)
        qd = (p["q"].astype(F32) * e_c).astype(BF16)
        tb = (p["dinv"] * p["b_r"]).astype(BF16)
        uw = _dot(tb, jnp.concatenate([p["v"], ke], axis=1))
        p["u"] = uw[:, 0:LANES]
        w = uw[:, LANES:2 * LANES].astype(BF16)
        p["wq"], p["kd"], p["egl"] = [], [], []
        for c in range(nchunk):
            rows = slice(c * CHUNK, (c + 1) * CHUNK)
            last = c * CHUNK if p["rev"] else (c + 1) * CHUNK - 1
            gl = p["gc_c"][last:last + 1, :]
            p["wq"].append(jnp.concatenate([w[rows], qd[rows]], axis=0))
            p["kd"].append((kf[rows] * jnp.exp(gl - p["gc_c"][rows])).astype(BF16))
            p["egl"].append(jnp.exp(gl))
        p["state"] = s_ref[p["d"], p["h"]]
        p["o1"] = [None] * nchunk

    for t in range(nchunk):
        for i, p in enumerate(probs):
            c = nchunk - 1 - t if p["rev"] else t
            rows = slice(c * CHUNK, (c + 1) * CHUNK)
            pq = _dot(p["wq"][c], p["state"].astype(BF16))
            vnew = (p["u"][rows] - pq[0:CHUNK]).astype(BF16)
            p["o1"][c] = pq[CHUNK:2 * CHUNK]
            vn_ref[i, rows, :] = vnew
            p["state"] = p["state"] * p["egl"][c] + lax.dot_general(
                p["kd"][c], vnew, (((0,), (0,)), ((), ())), preferred_element_type=F32)

    for i, p in enumerate(probs):
        s_ref[p["d"], p["h"]] = p["state"]
        o = jnp.concatenate(p["o1"], axis=0) + _dot(p["qkg"], vn_ref[i])
        rows = pl.ds(pl.multiple_of(p["blk"] * bt, bt), bt)
        o_ref[rows, p["h"] * LANES:(p["h"] + 1) * LANES] += o


def _deltanet(dnqkv, gcol, grow, B, S):
    T = dnqkv.shape[0]
    bt = DN_BT
    nb = S // bt
    fwd = lambda b, i: b * nb + i
    bwd = lambda b, i: b * nb + nb - 1 - i
    def specs(f):
        return [pl.BlockSpec((bt, 3 * DN_W), lambda b, i: (f(b, i), 0)),
                pl.BlockSpec((bt, 16), lambda b, i: (f(b, i), 0)),
                pl.BlockSpec((16, bt), lambda b, i: (0, f(b, i)))]
    return pl.pallas_call(
        _dn_kernel, grid=(B, nb),
        in_specs=specs(fwd) + specs(bwd),
        out_specs=pl.BlockSpec((S, DN_W), lambda b, i: (b, 0)),
        out_shape=jax.ShapeDtypeStruct((T, DN_W), F32),
        scratch_shapes=[pltpu.VMEM((2, N_DN, DK_DN, DK_DN), F32),
                        pltpu.VMEM((2 * N_DN, bt, DK_DN), BF16)],
        compiler_params=_cparams(("parallel", "arbitrary")), name="deltanet",
    )(dnqkv, gcol, grow, dnqkv, gcol, grow)


ATTN_TKB = 1024


def _attn_kernel(S, tq, tk, lam_init,
                 lam_ref, q_ref, k_ref, v_ref, g_ref, o_ref, s_scr, vaug):
    @pl.when(pl.program_id(2) == 0)
    def _():
        lane = lax.broadcasted_iota(jnp.int32, (S, LANES), 1)
        vaug[:, 0:LANES] = v_ref[...]
        vaug[:, LANES:2 * LANES] = jnp.where(lane == 0, 1.0, 0.0).astype(BF16)

    lf = lam_ref[...]
    lam = (jnp.exp(jnp.sum(lf[0:1] * lf[1:2], axis=-1, keepdims=True))
           - jnp.exp(jnp.sum(lf[2:3] * lf[3:4], axis=-1, keepdims=True)) + lam_init)
    q = q_ref[...]
    lane = lax.broadcasted_iota(jnp.int32, q.shape, 1)
    zero = jnp.zeros_like(q)
    qs = (jnp.where(lane < DH_DA, q, zero), jnp.where(lane < DH_DA, zero, q))

    m = []
    for c in range(2):
        mp = jnp.full((tq, LANES), -jnp.inf, F32)
        for i in range(S // tk):
            s = _dot_nt(qs[c], k_ref[i * tk:(i + 1) * tk, :])
            s_scr[c, :, i * tk:(i + 1) * tk] = s
            for j in range(tk // LANES):
                mp = jnp.maximum(mp, s[:, j * LANES:(j + 1) * LANES])
        m.append(jnp.max(mp, axis=-1, keepdims=True))

    outs = []
    for c in range(2):
        acc = jnp.zeros((tq, 2 * LANES), F32)
        for i in range(S // ATTN_TKB):
            sl = slice(i * ATTN_TKB, (i + 1) * ATTN_TKB)
            p = jnp.exp2(s_scr[c, :, sl] - m[c]).astype(BF16)
            acc = acc + _dot(p, vaug[sl, :])
        outs.append(acc)
    o1 = outs[0][:, 0:LANES] / outs[0][:, LANES:LANES + 1]
    o2 = outs[1][:, 0:LANES] / outs[1][:, LANES:LANES + 1]
    o_ref[...] = (_rms(o1 - lam * o2, g_ref[...]) * (1.0 - lam_init)).astype(BF16)


def _attn(aq, ak, av, da_lambda, subln_g, B, S, lam_init, tq=512, tk=512):
    T = aq.shape[0]
    nq = S // tq
    kern = functools.partial(_attn_kernel, S, tq, tk, lam_init)
    kv_spec = pl.BlockSpec((S, LANES), lambda b, h, i: (b, h))
    q_spec = pl.BlockSpec((tq, LANES), lambda b, h, i: (b * nq + i, h))
    return pl.pallas_call(
        kern, grid=(B, N_DA, nq),
        in_specs=[_const_spec(da_lambda.shape), q_spec, kv_spec, kv_spec,
                  _const_spec(subln_g.shape)],
        out_specs=q_spec,
        out_shape=jax.ShapeDtypeStruct((T, DN_W), BF16),
        scratch_shapes=[pltpu.VMEM((2, tq, S), F32), pltpu.VMEM((S, 2 * LANES), BF16)],
        compiler_params=_cparams(("parallel", "parallel", "arbitrary")), name="attn",
    )(da_lambda, aq, ak, av, subln_g)


FFN_FC = 256
FFN_NC = D_FF // FFN_FC


def _ffn_kernel(tiles_per_seq, tm, *refs):
    (xp_ref, x_ref, xn_ref, op_ref, o_ref, on_ref, zp_ref, z_ref, zn_ref,
     ap_ref, a_ref, an_ref, pe_ref,
     dng_ref, wout_ref, ln2_ref, wg_ref, wu_ref, cw_ref, cb_ref, wd_ref,
     pproj_ref, png_ref, pgn_ref, wpg_ref, out_ref, gbuf, ubuf, act_scr) = refs
    j = pl.program_id(0) % tiles_per_seq
    cat = lambda p, m, n: jnp.concatenate([p[...], m[...], n[...]], axis=0)

    o = cat(op_ref, o_ref, on_ref)
    z = cat(zp_ref, z_ref, zn_ref)
    dng = dng_ref[...]
    parts = []
    for h in range(N_DN):
        cols = slice(h * LANES, (h + 1) * LANES)
        parts.append((_rms(o[:, cols], dng) * _silu(z[:, cols].astype(F32))).astype(BF16))
    parts.append(cat(ap_ref, a_ref, an_ref))
    x1 = cat(xp_ref, x_ref, xn_ref) + _dot(jnp.concatenate(parts, axis=1), wout_ref[...])
    rowi = lax.broadcasted_iota(jnp.int32, (tm + 2 * HALO, 1), 0)
    pad = ((rowi < HALO) & (j == 0)) | ((rowi >= HALO + tm) & (j == tiles_per_seq - 1))
    x1 = jnp.where(pad, 0.0, x1)

    hn = _rms(x1, ln2_ref[...]).astype(BF16)
    hm = hn[HALO:HALO + tm]
    r = FFN_CONV // 2

    def matmuls(c, slot):
        gbuf[slot] = _dot(hn, wg_ref[c])
        ubuf[slot] = _dot(hm, wu_ref[c])

    def activate(c, slot):
        cw = cw_ref[c]
        gate = gbuf[slot, pl.ds(HALO - r, tm), :] * cw[0:1]
        for t in range(1, FFN_CONV):
            gate = gate + gbuf[slot, pl.ds(HALO - r + t, tm), :] * cw[t:t + 1]
        gate = gate + cb_ref[c]
        act = (_silu(gate) * ubuf[slot]).astype(BF16)
        act_scr[:, pl.ds(pl.multiple_of(c * FFN_FC, FFN_FC), FFN_FC)] = act

    def body(k, carry):
        matmuls(2 * k + 1, 1)
        activate(2 * k, 0)
        matmuls(2 * k + 2, 0)
        activate(2 * k + 1, 1)
        return carry

    assert FFN_NC % 2 == 1
    matmuls(0, 0)
    lax.fori_loop(0, FFN_NC // 2, body, 0)
    activate(FFN_NC - 1, 0)
    x2 = x1[HALO:HALO + tm] + _dot(act_scr[...], wd_ref[...])
    e = _rms(_dot(pe_ref[...].astype(BF16), pproj_ref[...]), png_ref[...])
    gt = jax.nn.sigmoid(_dot(_rms(x2, pgn_ref[...]).astype(BF16), wpg_ref[...]))
    out_ref[...] = x2 + gt * e


def _ffn(x, osum, z, oda, pe, S, prep, tm=512):
    T = x.shape[0]
    hb = tm // HALO
    nhb = T // HALO
    tps = S // tm
    kern = functools.partial(_ffn_kernel, tps, tm)
    tile = lambda w: pl.BlockSpec((tm, w), lambda i: (i, 0))

    def with_halo(w):
        return [pl.BlockSpec((HALO, w), lambda i: (jnp.maximum(i * hb - 1, 0), 0)), tile(w),
                pl.BlockSpec((HALO, w), lambda i: (jnp.minimum((i + 1) * hb, nhb - 1), 0))]

    names = ("dn_norm_g", "w_out", "ln2_g", "w_gate", "w_upp", "ffn_conv_w", "ffn_conv_b", "w_down",
             "ple_proj", "ple_norm_g", "ple_gate_norm_g", "w_ple_gate")
    in_specs = (with_halo(D_MODEL) + with_halo(DN_W) + with_halo(DN_W) + with_halo(DN_W)
                + [tile(PLE_DIM)]
                + [pl.BlockSpec(prep[n].shape, functools.partial(lambda nd, i: (0,) * nd, prep[n].ndim),
                                pipeline_mode=pl.Buffered(1)) for n in names])
    return pl.pallas_call(
        kern, grid=(T // tm,), in_specs=in_specs, out_specs=tile(D_MODEL),
        out_shape=jax.ShapeDtypeStruct((T, D_MODEL), F32),
        scratch_shapes=[pltpu.VMEM((2, tm + 2 * HALO, FFN_FC), F32),
                        pltpu.VMEM((2, tm, FFN_FC), F32),
                        pltpu.VMEM((tm, D_FF), BF16)],
        compiler_params=_cparams(("parallel",)), name="ffn",
    )(x, x, x, osum, osum, osum, z, z, z, oda, oda, oda, pe, *[prep[n] for n in names])


def _rope_tables(S):
    half = ROT_DIM // 2
    inv_freq = ROPE_THETA ** (-jnp.arange(0, ROT_DIM, 2, dtype=F32) / ROT_DIM)
    ang = jnp.arange(S, dtype=F32)[:, None] * inv_freq[None, :]
    cos, sin = jnp.cos(ang), jnp.sin(ang)
    zeros = jnp.zeros((S, DH_DA - ROT_DIM), F32)
    zh = jnp.zeros((S, half), F32)
    c = jnp.concatenate([cos, cos, zeros + 1.0], axis=1)
    s1 = jnp.concatenate([-sin, zh, zeros], axis=1)
    s2 = jnp.concatenate([zh, sin, zeros], axis=1)
    rep = lambda a: jnp.concatenate([a, a], axis=1)
    return rep(c), rep(s1), rep(s2)


def _prepare(layer, seqs, ln1_g, w_in, dn_conv_w, dn_a_log, dn_dt_bias, dn_norm_g, da_qk_norm_g,
             da_lambda, da_subln_g, w_out, ln2_g, w_up, ffn_conv_w, ffn_conv_b, w_down,
             ple_proj, ple_norm_g, ple_gate_norm_g, w_ple_gate):
    row = lambda a: a[layer].reshape(1, -1).astype(F32)
    w = w_in[layer]
    o_z, o_ab, o_aq = 3 * DN_W, 4 * DN_W, 4 * DN_W + 16
    w_rest = jnp.concatenate(
        [w[:, o_z:o_ab], w[:, o_aq:o_aq + 3 * DN_W], w[:, o_ab:o_aq],
         jnp.zeros((D_MODEL, LANES - 16), w.dtype)], axis=1)
    n8 = 2 * N_DN
    flat8 = lambda a: a[layer].reshape(n8).astype(F32)
    both = lambda a: jnp.zeros((n8, 2 * LANES), F32).at[0, 0:n8].set(flat8(a)).at[:, LANES].set(flat8(a))
    chunked = lambda a: a.reshape(a.shape[0], FFN_NC, FFN_FC).transpose(1, 0, 2)
    wu = w_up[layer]
    return {
        "ln1_g": row(ln1_g),
        "w_dn": w[:, 0:3 * DN_W].astype(BF16),
        "w_rest": w_rest.astype(BF16),
        "dn_conv_w": dn_conv_w[layer].astype(F32),
        "alog": both(dn_a_log), "dtb": both(dn_dt_bias),
        "qkg": jnp.tile(da_qk_norm_g[layer].astype(F32), (1, 2)),
        "rope": {S: _rope_tables(S) for S in seqs},
        "dn_norm_g": row(dn_norm_g),
        "da_lambda": da_lambda[layer].astype(F32),
        "da_subln_g": row(da_subln_g),
        "w_out": w_out[layer].astype(BF16),
        "ln2_g": row(ln2_g),
        "w_gate": chunked(wu[:, 0:D_FF]).astype(BF16),
        "w_upp": chunked(wu[:, D_FF:]).astype(BF16),
        "ffn_conv_w": chunked(ffn_conv_w[layer].astype(F32)),
        "ffn_conv_b": ffn_conv_b[layer].astype(F32).reshape(FFN_NC, 1, FFN_FC),
        "w_down": w_down[layer].astype(BF16),
        "ple_proj": ple_proj[layer].astype(BF16),
        "ple_norm_g": row(ple_norm_g),
        "ple_gate_norm_g": row(ple_gate_norm_g),
        "w_ple_gate": w_ple_gate[layer].astype(BF16),
    }


def _encoder_layer(x, pe, layer, prep):
    B, S, _ = x.shape
    lam_init = 0.8 - 0.6 * math.exp(-0.3 * layer)
    xf = x.reshape(B * S, D_MODEL)
    dnqkv, z, gcol, grow, aq, ak, av = _inproj(xf, S, prep)
    osum = _deltanet(dnqkv, gcol, grow, B, S)
    oda = _attn(aq, ak, av, prep["da_lambda"], prep["da_subln_g"], B, S, lam_init)
    out = _ffn(xf, osum, z, oda, pe.reshape(B * S, PLE_DIM), S, prep)
    return out.reshape(B, S, D_MODEL)


def kernel(x_prompt, x_sample, p_prompt, p_sample, ln1_g, w_in, dn_conv_w, dn_a_log, dn_dt_bias, dn_norm_g, da_qk_norm_g, da_lambda, da_subln_g, w_out, ln2_g, w_up, ffn_conv_w, ffn_conv_b, w_down, ple_proj, ple_norm_g, ple_gate_norm_g, w_ple_gate):
    weights = (ln1_g, w_in, dn_conv_w, dn_a_log, dn_dt_bias, dn_norm_g, da_qk_norm_g, da_lambda,
               da_subln_g, w_out, ln2_g, w_up, ffn_conv_w, ffn_conv_b, w_down, ple_proj,
               ple_norm_g, ple_gate_norm_g, w_ple_gate)
    depth = w_in.shape[0]
    seqs = (x_prompt.shape[1], x_sample.shape[1])
    outs = []
    for x, p in ((x_prompt, p_prompt), (x_sample, p_sample)):
        for layer in range(depth):
            prep = _prepare(layer, seqs, *weights)
            x = _encoder_layer(x, p[layer], layer, prep)
        outs.append(x)
    return tuple(outs)
```

```python
import functools
import math

import jax
import jax.numpy as jnp
from jax import lax
from jax.experimental import pallas as pl
from jax.experimental.pallas import tpu as pltpu

D_MODEL = 1024
N_DN = 4
DK_DN = 128
DN_CONV = 5
CHUNK = 64
N_DA = 4
DH_DA = 64
ROT_DIM = DH_DA // 4
ROPE_THETA = 500000.0
D_FF = 2816
FFN_CONV = 3
PLE_DIM = 256
EPS = 1e-6
DN_W = N_DN * DK_DN
LANES = 128
HALO = 16
VMEM_LIMIT = 56 * 1024 * 1024

F32 = jnp.float32
BF16 = jnp.bfloat16
LOG2E = math.log2(math.e)


def _dot(a, b):
    return jnp.dot(a, b, preferred_element_type=F32)


def _dot_nt(a, b):
    return lax.dot_general(a, b, (((1,), (1,)), ((), ())), preferred_element_type=F32)


def _rms(x, g):
    return x * lax.rsqrt(jnp.mean(x * x, axis=-1, keepdims=True) + EPS) * g


def _silu(x):
    return x * jax.nn.sigmoid(x)


def _cparams(sem):
    return pltpu.CompilerParams(dimension_semantics=sem, vmem_limit_bytes=VMEM_LIMIT)


def _const_spec(shape):
    nd = len(shape)
    return pl.BlockSpec(shape, lambda *_: (0,) * nd)


def _row_groups(tm):
    groups = ((0, 36), (288, 28))
    assert tm == sum(8 * s for _, s in groups)
    return groups


def _inproj_kernel(tiles_per_seq, tm,
                   xp_ref, x_ref, xn_ref, ln_ref, wdn_ref, wrest_ref, convw_ref,
                   alog_ref, dtb_ref, qkg_ref, rc_ref, rs1_ref, rs2_ref,
                   dn_ref, z_ref, gcol_ref, grow_ref, aq_ref, ak_ref, av_ref, pbuf, ybuf):
    j = pl.program_id(0) % tiles_per_seq
    ln = ln_ref[...]
    xp = jnp.where(j == 0, 0.0, xp_ref[...])
    xn = jnp.where(j == tiles_per_seq - 1, 0.0, xn_ref[...])
    xx = jnp.concatenate([xp, x_ref[...], xn], axis=0)
    hn = _rms(xx, ln).astype(BF16)
    hm = hn[HALO:HALO + tm]

    wrest = wrest_ref
    lane = lax.broadcasted_iota(jnp.int32, (tm, LANES), 1)
    lo = lane < DH_DA
    rc, rs1, rs2 = rc_ref[...], rs1_ref[...], rs2_ref[...]
    half = ROT_DIM // 2
    for which, out_ref in ((0, aq_ref), (1, ak_ref)):
        base = (1 + which) * DN_W
        g = qkg_ref[which:which + 1, :]
        for h2 in range(N_DA // 2):
            pair = _dot(hm, wrest[:, base + h2 * 2 * LANES:base + (h2 + 1) * 2 * LANES])
            for h in (2 * h2, 2 * h2 + 1):
                a = pair[:, (h % 2) * LANES:(h % 2 + 1) * LANES]
                sq = a * a
                s_lo = jnp.sum(jnp.where(lo, sq, 0.0), axis=-1, keepdims=True)
                s_hi = jnp.sum(jnp.where(lo, 0.0, sq), axis=-1, keepdims=True)
                inv = lax.rsqrt(jnp.where(lo, s_lo, s_hi) * (1.0 / DH_DA) + EPS)
                a = a * inv * g
                a = (a * rc + pltpu.roll(a, LANES - half, 1) * rs1
                     + pltpu.roll(a, half, 1) * rs2)
                if which == 0:
                    a = a * (DH_DA ** -0.5 * LOG2E)
                out_ref[:, h * LANES:(h + 1) * LANES] = a.astype(BF16)

    ab = _dot(hm, wrest[:, 4 * DN_W:4 * DN_W + LANES])
    abt = ab.T
    n8 = 2 * N_DN

    def gates(da, db, alog, dtb):
        g = -jnp.exp(alog) * jax.nn.softplus(da + dtb)
        return g, jax.nn.sigmoid(db)

    gc, bc = gates(ab[:, 0:n8], ab[:, n8:2 * n8], alog_ref[0:1, 0:n8], dtb_ref[0:1, 0:n8])
    gcol_ref[...] = jnp.concatenate([gc, bc], axis=1)
    gr, br = gates(abt[0:n8], abt[n8:2 * n8], alog_ref[:, LANES:LANES + 1][0:n8],
                   dtb_ref[:, LANES:LANES + 1][0:n8])
    grow_ref[...] = jnp.concatenate([gr, br], axis=0)

    for c2 in range(3 * N_DN // 2):
        res = _dot(hn, wdn_ref[:, c2 * 2 * LANES:(c2 + 1) * 2 * LANES])
        pbuf[2 * c2] = res[:, 0:LANES]
        pbuf[2 * c2 + 1] = res[:, LANES:2 * LANES]
    r = DN_CONV // 2
    for c in range(3 * N_DN):
        cols = slice(c * LANES, (c + 1) * LANES)
        taps = [convw_ref[t:t + 1, cols] for t in range(DN_CONV)]
        pieces = []
        for r0, s in _row_groups(tm):
            for b in range(s):
                acc = pbuf[c, pl.ds(HALO - r + r0 + b, 8, stride=s), :] * taps[0]
                for t in range(1, DN_CONV):
                    acc = acc + pbuf[c, pl.ds(HALO - r + t + r0 + b, 8, stride=s), :] * taps[t]
                pieces.append(acc)
        y = _silu(jnp.concatenate(pieces, axis=0))
        if c < 2 * N_DN:
            y = y * lax.rsqrt(jnp.sum(y * y, axis=-1, keepdims=True) + EPS)
            if c < N_DN:
                y = y * (DK_DN ** -0.5)
        n = 0
        for r0, s in _row_groups(tm):
            for b in range(s):
                ybuf[c, pl.ds(r0 + b, 8, stride=s), :] = y[8 * n:8 * n + 8]
                n += 1
        dn_ref[:, cols] = ybuf[c].astype(BF16)

    z_ref[...] = _dot(hm, wrest[:, 0:DN_W]).astype(BF16)
    av_ref[...] = _dot(hm, wrest[:, 3 * DN_W:4 * DN_W]).astype(BF16)


def _inproj(x, S, prep, tm=512):
    T = x.shape[0]
    assert S % tm == 0 and T % S == 0
    nt = T // tm
    hb = tm // HALO
    nhb = T // HALO
    tps = S // tm
    kern = functools.partial(_inproj_kernel, tps, tm)
    out_shape = (
        jax.ShapeDtypeStruct((T, 3 * DN_W), BF16),
        jax.ShapeDtypeStruct((T, DN_W), BF16),
        jax.ShapeDtypeStruct((T, 16), F32),
        jax.ShapeDtypeStruct((16, T), F32),
        jax.ShapeDtypeStruct((T, DN_W), BF16),
        jax.ShapeDtypeStruct((T, DN_W), BF16),
        jax.ShapeDtypeStruct((T, DN_W), BF16),
    )
    tile = lambda w: pl.BlockSpec((tm, w), lambda i: (i, 0))
    in_specs = [
        pl.BlockSpec((HALO, D_MODEL), lambda i: (jnp.maximum(i * hb - 1, 0), 0)),
        tile(D_MODEL),
        pl.BlockSpec((HALO, D_MODEL), lambda i: (jnp.minimum((i + 1) * hb, nhb - 1), 0)),
        _const_spec((1, D_MODEL)),
        _const_spec(prep["w_dn"].shape),
        _const_spec(prep["w_rest"].shape),
        _const_spec(prep["dn_conv_w"].shape),
        _const_spec(prep["alog"].shape),
        _const_spec(prep["dtb"].shape),
        _const_spec(prep["qkg"].shape),
        pl.BlockSpec((tm, LANES), lambda i: (i % tps, 0)),
        pl.BlockSpec((tm, LANES), lambda i: (i % tps, 0)),
        pl.BlockSpec((tm, LANES), lambda i: (i % tps, 0)),
    ]
    out_specs = (tile(3 * DN_W), tile(DN_W), tile(16),
                 pl.BlockSpec((16, tm), lambda i: (0, i)),
                 tile(DN_W), tile(DN_W), tile(DN_W))
    rc, rs1, rs2 = prep["rope"][S]
    return pl.pallas_call(
        kern, grid=(nt,), in_specs=in_specs, out_specs=out_specs, out_shape=out_shape,
        scratch_shapes=[pltpu.VMEM((3 * N_DN, tm + 2 * HALO, LANES), F32),
                        pltpu.VMEM((3 * N_DN, tm, LANES), F32)],
        compiler_params=_cparams(("parallel",)), name="inproj",
    )(x, x, x, prep["ln1_g"], prep["w_dn"], prep["w_rest"], prep["dn_conv_w"],
      prep["alog"], prep["dtb"], prep["qkg"], rc, rs1, rs2)


DN_SUB = 256
DN_BT = 512


def _chunk_cumsum(x, axis, reverse):
    n = x.shape[axis]
    pos = lax.broadcasted_iota(jnp.int32, x.shape, axis) % CHUNK
    s = 1
    while s < CHUNK:
        if reverse:
            sh = pltpu.roll(x, n - s, axis)
            x = x + jnp.where(pos < CHUNK - s, sh, 0.0)
        else:
            sh = pltpu.roll(x, s, axis)
            x = x + jnp.where(pos >= s, sh, 0.0)
        s *= 2
    return x


def _dn_kernel(qkvf_ref, gcf_ref, grf_ref, qkvb_ref, gcb_ref, grb_ref,
               o_ref, s_ref, vn_ref):
    bt = DN_SUB
    nchunk = bt // CHUNK
    nstage = DN_BT // DN_SUB
    step = pl.program_id(1)
    nb = pl.num_programs(1)

    @pl.when(step == 0)
    def _():
        s_ref[...] = jnp.zeros_like(s_ref)
        o_ref[...] = jnp.zeros_like(o_ref)

    row = lax.broadcasted_iota(jnp.int32, (bt, bt), 0)
    col = lax.broadcasted_iota(jnp.int32, (bt, bt), 1)
    same = (row // CHUNK) == (col // CHUNK)
    eye = (row == col).astype(F32)

    def level(b):
        return ((row // (2 * b)) == (col // (2 * b))) & ((row // b) != (col // b))

    def run_stage(stage):
        probs = []
        dirs = ((qkvf_ref, gcf_ref, grf_ref, step), (qkvb_ref, gcb_ref, grb_ref, nb - 1 - step))
        for d, (qkv_ref, gc_ref, gr_ref, blk) in enumerate(dirs):
            rev = d == 1
            sub = nstage - 1 - stage if rev else stage
            tok = slice(sub * bt, (sub + 1) * bt)
            gcol = gc_ref[tok, :]
            grow = gr_ref[:, tok]
            cum_c = _chunk_cumsum(gcol, 0, rev)
            cum_r = _chunk_cumsum(grow, 1, rev)
            if rev:
                incl = same & (row <= col)
                strict = same & (row < col)
            else:
                incl = same & (row >= col)
                strict = same & (row > col)
            for h in range(N_DN):
                ci = d * N_DN + h
                probs.append(dict(
                    d=d, h=h, rev=rev, row0=blk * DN_BT + sub * bt, incl=incl, strict=strict,
                    gc_c=cum_c[:, ci:ci + 1], gc_r=cum_r[ci:ci + 1, :],
                    b_c=gcol[:, 2 * N_DN + ci:2 * N_DN + ci + 1],
                    b_r=grow[2 * N_DN + ci:2 * N_DN + ci + 1, :],
                    q=qkv_ref[tok, h * LANES:(h + 1) * LANES],
                    k=qkv_ref[tok, DN_W + h * LANES:DN_W + (h + 1) * LANES],
                    v=qkv_ref[tok, 2 * DN_W + h * LANES:2 * DN_W + (h + 1) * LANES]))

        for p in probs:
            gam = jnp.exp(jnp.where(p["incl"], p["gc_c"] - p["gc_r"], -jnp.inf))
            kk = _dot_nt(p["k"], p["k"])
            qk = _dot_nt(p["q"], p["k"])
            p["a"] = jnp.where(p["strict"], p["b_c"] * kk * gam, 0.0)
            p["qkg"] = (qk * gam).astype(BF16)
            p["dinv"] = eye - jnp.where(level(1), p["a"], 0.0)

        b = 2
        while b < CHUNK:
            lm = level(b)
            for p in probs:
                p["db"] = p["dinv"].astype(BF16)
                p["x"] = _dot(p["db"], jnp.where(lm, p["a"], 0.0).astype(BF16)).astype(BF16)
            for p in probs:
                p["dinv"] = p["dinv"] - _dot(p["x"], p["db"])
            b *= 2

        for p in probs:
            e_c = jnp.exp(p["gc_c"])
            kf = p["k"].astype(F32)
            ke = (kf * e_c).astype(BF16)
            qd = (p["q"].astype(F32) * e_c).astype(BF16)
            tb = (p["dinv"] * p["b_r"]).astype(BF16)
            uw = _dot(tb, jnp.concatenate([p["v"], ke], axis=1))
            p["u"] = uw[:, 0:LANES]
            w = uw[:, LANES:2 * LANES].astype(BF16)
            p["wq"], p["kd"], p["egl"] = [], [], []
            for c in range(nchunk):
                rows = slice(c * CHUNK, (c + 1) * CHUNK)
                last = c * CHUNK if p["rev"] else (c + 1) * CHUNK - 1
                gl = p["gc_c"][last:last + 1, :]
                p["wq"].append(jnp.concatenate([w[rows], qd[rows]], axis=0))
                p["kd"].append((kf[rows] * jnp.exp(gl - p["gc_c"][rows])).astype(BF16))
                p["egl"].append(jnp.exp(gl))
            p["state"] = s_ref[p["d"], p["h"]]
            p["o1"] = [None] * nchunk

        for t in range(nchunk):
            for i, p in enumerate(probs):
                c = nchunk - 1 - t if p["rev"] else t
                rows = slice(c * CHUNK, (c + 1) * CHUNK)
                pq = _dot(p["wq"][c], p["state"].astype(BF16))
                vnew = (p["u"][rows] - pq[0:CHUNK]).astype(BF16)
                p["o1"][c] = pq[CHUNK:2 * CHUNK]
                vn_ref[stage, i, rows, :] = vnew
                p["state"] = p["state"] * p["egl"][c] + lax.dot_general(
                    p["kd"][c], vnew, (((0,), (0,)), ((), ())), preferred_element_type=F32)

        for i, p in enumerate(probs):
            s_ref[p["d"], p["h"]] = p["state"]
            o = jnp.concatenate(p["o1"], axis=0) + _dot(p["qkg"], vn_ref[stage, i])
            rows = pl.ds(pl.multiple_of(p["row0"], bt), bt)
            o_ref[rows, p["h"] * LANES:(p["h"] + 1) * LANES] += o

    for stage in range(nstage):
        run_stage(stage)


def _deltanet(dnqkv, gcol, grow, B, S):
    T = dnqkv.shape[0]
    bt = DN_BT
    nb = S // bt
    fwd = lambda b, i: b * nb + i
    bwd = lambda b, i: b * nb + nb - 1 - i
    def specs(f):
        return [pl.BlockSpec((bt, 3 * DN_W), lambda b, i: (f(b, i), 0)),
                pl.BlockSpec((bt, 16), lambda b, i: (f(b, i), 0)),
                pl.BlockSpec((16, bt), lambda b, i: (0, f(b, i)))]
    return pl.pallas_call(
        _dn_kernel, grid=(B, nb),
        in_specs=specs(fwd) + specs(bwd),
        out_specs=pl.BlockSpec((S, DN_W), lambda b, i: (b, 0)),
        out_shape=jax.ShapeDtypeStruct((T, DN_W), F32),
        scratch_shapes=[pltpu.VMEM((2, N_DN, DK_DN, DK_DN), F32),
                        pltpu.VMEM((DN_BT // DN_SUB, 2 * N_DN, DN_SUB, DK_DN), BF16)],
        compiler_params=_cparams(("parallel", "arbitrary")), name="deltanet",
    )(dnqkv, gcol, grow, dnqkv, gcol, grow)


ATTN_TKB = 1024


def _attn_kernel(S, tq, tk, lam_init,
                 lam_ref, q_ref, k_ref, v_ref, g_ref, o_ref, s_scr, vaug):
    @pl.when(pl.program_id(2) == 0)
    def _():
        vaug[:, 0:LANES] = v_ref[...]
        vaug[:, LANES:2 * LANES] = jnp.ones((S, LANES), BF16)

    lf = lam_ref[...]
    lam = (jnp.exp(jnp.sum(lf[0:1] * lf[1:2], axis=-1, keepdims=True))
           - jnp.exp(jnp.sum(lf[2:3] * lf[3:4], axis=-1, keepdims=True)) + lam_init)
    q = q_ref[...]
    lane = lax.broadcasted_iota(jnp.int32, q.shape, 1)
    zero = jnp.zeros_like(q)
    qs = (jnp.where(lane < DH_DA, q, zero), jnp.where(lane < DH_DA, zero, q))

    m = []
    for c in range(2):
        mp = jnp.full((tq, LANES), -jnp.inf, F32)
        for i in range(S // tk):
            s = _dot_nt(qs[c], k_ref[i * tk:(i + 1) * tk, :])
            s_scr[c, :, i * tk:(i + 1) * tk] = s
            for j in range(tk // LANES):
                mp = jnp.maximum(mp, s[:, j * LANES:(j + 1) * LANES])
        m.append(jnp.max(mp, axis=-1, keepdims=True))

    outs = []
    for c in range(2):
        acc = jnp.zeros((tq, 2 * LANES), F32)
        for i in range(S // ATTN_TKB):
            sl = slice(i * ATTN_TKB, (i + 1) * ATTN_TKB)
            p = jnp.exp2(s_scr[c, :, sl] - m[c]).astype(BF16)
            acc = acc + _dot(p, vaug[sl, :])
        outs.append(acc)
    o1 = outs[0][:, 0:LANES] / outs[0][:, LANES:2 * LANES]
    o2 = outs[1][:, 0:LANES] / outs[1][:, LANES:2 * LANES]
    o_ref[...] = (_rms(o1 - lam * o2, g_ref[...]) * (1.0 - lam_init)).astype(BF16)


def _attn(aq, ak, av, da_lambda, subln_g, B, S, lam_init, tq=512, tk=512):
    T = aq.shape[0]
    nq = S // tq
    kern = functools.partial(_attn_kernel, S, tq, tk, lam_init)
    kv_spec = pl.BlockSpec((S, LANES), lambda b, h, i: (b, h))
    q_spec = pl.BlockSpec((tq, LANES), lambda b, h, i: (b * nq + i, h))
    return pl.pallas_call(
        kern, grid=(B, N_DA, nq),
        in_specs=[_const_spec(da_lambda.shape), q_spec, kv_spec, kv_spec,
                  _const_spec(subln_g.shape)],
        out_specs=q_spec,
        out_shape=jax.ShapeDtypeStruct((T, DN_W), BF16),
        scratch_shapes=[pltpu.VMEM((2, tq, S), F32), pltpu.VMEM((S, 2 * LANES), BF16)],
        compiler_params=_cparams(("parallel", "parallel", "arbitrary")), name="attn",
    )(da_lambda, aq, ak, av, subln_g)


FFN_FC = 256
FFN_NC = D_FF // FFN_FC


def _ffn_kernel(tiles_per_seq, tm, *refs):
    (xp_ref, x_ref, xn_ref, op_ref, o_ref, on_ref, zp_ref, z_ref, zn_ref,
     ap_ref, a_ref, an_ref, pe_ref,
     dng_ref, wout_ref, ln2_ref, wg_ref, wu_ref, cw_ref, cb_ref, wd_ref,
     pproj_ref, png_ref, pgn_ref, wpg_ref, out_ref, gbuf, ubuf, act_scr) = refs
    j = pl.program_id(0) % tiles_per_seq
    cat = lambda p, m, n: jnp.concatenate([p[...], m[...], n[...]], axis=0)

    o = cat(op_ref, o_ref, on_ref)
    z = cat(zp_ref, z_ref, zn_ref)
    dng = dng_ref[...]
    parts = []
    for h in range(N_DN):
        cols = slice(h * LANES, (h + 1) * LANES)
        parts.append((_rms(o[:, cols], dng) * _silu(z[:, cols].astype(F32))).astype(BF16))
    parts.append(cat(ap_ref, a_ref, an_ref))
    x1 = cat(xp_ref, x_ref, xn_ref) + _dot(jnp.concatenate(parts, axis=1), wout_ref[...])
    rowi = lax.broadcasted_iota(jnp.int32, (tm + 2 * HALO, 1), 0)
    pad = ((rowi < HALO) & (j == 0)) | ((rowi >= HALO + tm) & (j == tiles_per_seq - 1))
    x1 = jnp.where(pad, 0.0, x1)

    hn = _rms(x1, ln2_ref[...]).astype(BF16)
    hm = hn[HALO:HALO + tm]
    r = FFN_CONV // 2

    def matmuls(c, slot):
        gbuf[slot] = _dot(hn, wg_ref[c])
        ubuf[slot] = _dot(hm, wu_ref[c])

    def activate(c, slot):
        cw = cw_ref[c]
        gate = gbuf[slot, pl.ds(HALO - r, tm), :] * cw[0:1]
        for t in range(1, FFN_CONV):
            gate = gate + gbuf[slot, pl.ds(HALO - r + t, tm), :] * cw[t:t + 1]
        gate = gate + cb_ref[c]
        act = (_silu(gate) * ubuf[slot]).astype(BF16)
        act_scr[:, pl.ds(pl.multiple_of(c * FFN_FC, FFN_FC), FFN_FC)] = act

    def body(k, carry):
        matmuls(2 * k + 1, 1)
        activate(2 * k, 0)
        matmuls(2 * k + 2, 0)
        activate(2 * k + 1, 1)
        return carry

    assert FFN_NC % 2 == 1
    matmuls(0, 0)
    lax.fori_loop(0, FFN_NC // 2, body, 0)
    activate(FFN_NC - 1, 0)
    x2 = x1[HALO:HALO + tm] + _dot(act_scr[...], wd_ref[...])
    e = _rms(_dot(pe_ref[...].astype(BF16), pproj_ref[...]), png_ref[...])
    gt = jax.nn.sigmoid(_dot(_rms(x2, pgn_ref[...]).astype(BF16), wpg_ref[...]))
    out_ref[...] = x2 + gt * e


def _ffn(x, osum, z, oda, pe, S, prep, tm=512):
    T = x.shape[0]
    hb = tm // HALO
    nhb = T // HALO
    tps = S // tm
    kern = functools.partial(_ffn_kernel, tps, tm)
    tile = lambda w: pl.BlockSpec((tm, w), lambda i: (i, 0))

    def with_halo(w):
        return [pl.BlockSpec((HALO, w), lambda i: (jnp.maximum(i * hb - 1, 0), 0)), tile(w),
                pl.BlockSpec((HALO, w), lambda i: (jnp.minimum((i + 1) * hb, nhb - 1), 0))]

    names = ("dn_norm_g", "w_out", "ln2_g", "w_gate", "w_upp", "ffn_conv_w", "ffn_conv_b", "w_down",
             "ple_proj", "ple_norm_g", "ple_gate_norm_g", "w_ple_gate")
    in_specs = (with_halo(D_MODEL) + with_halo(DN_W) + with_halo(DN_W) + with_halo(DN_W)
                + [tile(PLE_DIM)]
                + [pl.BlockSpec(prep[n].shape, functools.partial(lambda nd, i: (0,) * nd, prep[n].ndim),
                                pipeline_mode=pl.Buffered(1)) for n in names])
    return pl.pallas_call(
        kern, grid=(T // tm,), in_specs=in_specs, out_specs=tile(D_MODEL),
        out_shape=jax.ShapeDtypeStruct((T, D_MODEL), F32),
        scratch_shapes=[pltpu.VMEM((2, tm + 2 * HALO, FFN_FC), F32),
                        pltpu.VMEM((2, tm, FFN_FC), F32),
                        pltpu.VMEM((tm, D_FF), BF16)],
        compiler_params=_cparams(("parallel",)), name="ffn",
    )(x, x, x, osum, osum, osum, z, z, z, oda, oda, oda, pe, *[prep[n] for n in names])


def _rope_tables(S):
    half = ROT_DIM // 2
    inv_freq = ROPE_THETA ** (-jnp.arange(0, ROT_DIM, 2, dtype=F32) / ROT_DIM)
    ang = jnp.arange(S, dtype=F32)[:, None] * inv_freq[None, :]
    cos, sin = jnp.cos(ang), jnp.sin(ang)
    zeros = jnp.zeros((S, DH_DA - ROT_DIM), F32)
    zh = jnp.zeros((S, half), F32)
    c = jnp.concatenate([cos, cos, zeros + 1.0], axis=1)
    s1 = jnp.concatenate([-sin, zh, zeros], axis=1)
    s2 = jnp.concatenate([zh, sin, zeros], axis=1)
    rep = lambda a: jnp.concatenate([a, a], axis=1)
    return rep(c), rep(s1), rep(s2)


def _prepare(layer, seqs, ln1_g, w_in, dn_conv_w, dn_a_log, dn_dt_bias, dn_norm_g, da_qk_norm_g,
             da_lambda, da_subln_g, w_out, ln2_g, w_up, ffn_conv_w, ffn_conv_b, w_down,
             ple_proj, ple_norm_g, ple_gate_norm_g, w_ple_gate):
    row = lambda a: a[layer].reshape(1, -1).astype(F32)
    w = w_in[layer]
    o_z, o_ab, o_aq = 3 * DN_W, 4 * DN_W, 4 * DN_W + 16
    w_rest = jnp.concatenate(
        [w[:, o_z:o_ab], w[:, o_aq:o_aq + 3 * DN_W], w[:, o_ab:o_aq],
         jnp.zeros((D_MODEL, LANES - 16), w.dtype)], axis=1)
    n8 = 2 * N_DN
    flat8 = lambda a: a[layer].reshape(n8).astype(F32)
    both = lambda a: jnp.zeros((n8, 2 * LANES), F32).at[0, 0:n8].set(flat8(a)).at[:, LANES].set(flat8(a))
    chunked = lambda a: a.reshape(a.shape[0], FFN_NC, FFN_FC).transpose(1, 0, 2)
    wu = w_up[layer]
    return {
        "ln1_g": row(ln1_g),
        "w_dn": w[:, 0:3 * DN_W].astype(BF16),
        "w_rest": w_rest.astype(BF16),
        "dn_conv_w": dn_conv_w[layer].astype(F32),
        "alog": both(dn_a_log), "dtb": both(dn_dt_bias),
        "qkg": jnp.tile(da_qk_norm_g[layer].astype(F32), (1, 2)),
        "rope": {S: _rope_tables(S) for S in seqs},
        "dn_norm_g": row(dn_norm_g),
        "da_lambda": da_lambda[layer].astype(F32),
        "da_subln_g": row(da_subln_g),
        "w_out": w_out[layer].astype(BF16),
        "ln2_g": row(ln2_g),
        "w_gate": chunked(wu[:, 0:D_FF]).astype(BF16),
        "w_upp": chunked(wu[:, D_FF:]).astype(BF16),
        "ffn_conv_w": chunked(ffn_conv_w[layer].astype(F32)),
        "ffn_conv_b": ffn_conv_b[layer].astype(F32).reshape(FFN_NC, 1, FFN_FC),
        "w_down": w_down[layer].astype(BF16),
        "ple_proj": ple_proj[layer].astype(BF16),
        "ple_norm_g": row(ple_norm_g),
        "ple_gate_norm_g": row(ple_gate_norm_g),
        "w_ple_gate": w_ple_gate[layer].astype(BF16),
    }


def _encoder_layer(x, pe, layer, prep):
    B, S, _ = x.shape
    lam_init = 0.8 - 0.6 * math.exp(-0.3 * layer)
    xf = x.reshape(B * S, D_MODEL)
    dnqkv, z, gcol, grow, aq, ak, av = _inproj(xf, S, prep)
    osum = _deltanet(dnqkv, gcol, grow, B, S)
    oda = _attn(aq, ak, av, prep["da_lambda"], prep["da_subln_g"], B, S, lam_init)
    out = _ffn(xf, osum, z, oda, pe.reshape(B * S, PLE_DIM), S, prep)
    return out.reshape(B, S, D_MODEL)


def kernel(x_prompt, x_sample, p_prompt, p_sample, ln1_g, w_in, dn_conv_w, dn_a_log, dn_dt_bias, dn_norm_g, da_qk_norm_g, da_lambda, da_subln_g, w_out, ln2_g, w_up, ffn_conv_w, ffn_conv_b, w_down, ple_proj, ple_norm_g, ple_gate_norm_g, w_ple_gate):
    weights = (ln1_g, w_in, dn_conv_w, dn_a_log, dn_dt_bias, dn_norm_g, da_qk_norm_g, da_lambda,
               da_subln_g, w_out, ln2_g, w_up, ffn_conv_w, ffn_conv_b, w_down, ple_proj,
               ple_norm_g, ple_gate_norm_g, w_ple_gate)
    depth = w_in.shape[0]
    seqs = (x_prompt.shape[1], x_sample.shape[1])
    outs = []
    for x, p in ((x_prompt, p_prompt), (x_sample, p_sample)):
        for layer in range(depth):
            prep = _prepare(layer, seqs, *weights)
            x = _encoder_layer(x, p[layer], layer, prep)
        outs.append(x)
    return tuple(outs)
```

```python
import functools
import math

import jax
import jax.numpy as jnp
from jax import lax
from jax.experimental import pallas as pl
from jax.experimental.pallas import tpu as pltpu

D_MODEL = 1024
N_DN = 4
DK_DN = 128
DN_CONV = 5
CHUNK = 64
N_DA = 4
DH_DA = 64
ROT_DIM = DH_DA // 4
ROPE_THETA = 500000.0
D_FF = 2816
FFN_CONV = 3
PLE_DIM = 256
EPS = 1e-6
DN_W = N_DN * DK_DN
LANES = 128
HALO = 16
VMEM_LIMIT = 56 * 1024 * 1024

F32 = jnp.float32
BF16 = jnp.bfloat16
LOG2E = math.log2(math.e)


def _dot(a, b):
    return jnp.dot(a, b, preferred_element_type=F32)


def _dot_nt(a, b):
    return lax.dot_general(a, b, (((1,), (1,)), ((), ())), preferred_element_type=F32)


def _rms(x, g):
    return x * lax.rsqrt(jnp.mean(x * x, axis=-1, keepdims=True) + EPS) * g


def _silu(x):
    return x * jax.nn.sigmoid(x)


def _cparams(sem):
    return pltpu.CompilerParams(dimension_semantics=sem, vmem_limit_bytes=VMEM_LIMIT)


def _const_spec(shape):
    nd = len(shape)
    return pl.BlockSpec(shape, lambda *_: (0,) * nd)


def _row_groups(tm):
    groups = ((0, 36), (288, 28))
    assert tm == sum(8 * s for _, s in groups)
    return groups


def _inproj_kernel(tiles_per_seq, tm,
                   xp_ref, x_ref, xn_ref, ln_ref, wdn_ref, wrest_ref, convw_ref,
                   alog_ref, dtb_ref, qkg_ref, rc_ref, rs1_ref, rs2_ref,
                   dn_ref, z_ref, gcol_ref, grow_ref, aq_ref, ak_ref, av_ref, pbuf, ybuf):
    j = pl.program_id(0) % tiles_per_seq
    ln = ln_ref[...]
    xp = jnp.where(j == 0, 0.0, xp_ref[...])
    xn = jnp.where(j == tiles_per_seq - 1, 0.0, xn_ref[...])
    xx = jnp.concatenate([xp, x_ref[...], xn], axis=0)
    hn = _rms(xx, ln).astype(BF16)
    hm = hn[HALO:HALO + tm]

    wrest = wrest_ref
    lane = lax.broadcasted_iota(jnp.int32, (tm, LANES), 1)
    lo = lane < DH_DA
    rc, rs1, rs2 = rc_ref[...], rs1_ref[...], rs2_ref[...]
    half = ROT_DIM // 2
    for which, out_ref in ((0, aq_ref), (1, ak_ref)):
        base = (1 + which) * DN_W
        g = qkg_ref[which:which + 1, :]
        for h2 in range(N_DA // 2):
            pair = _dot(hm, wrest[:, base + h2 * 2 * LANES:base + (h2 + 1) * 2 * LANES])
            for h in (2 * h2, 2 * h2 + 1):
                a = pair[:, (h % 2) * LANES:(h % 2 + 1) * LANES]
                sq = a * a
                s_lo = jnp.sum(jnp.where(lo, sq, 0.0), axis=-1, keepdims=True)
                s_hi = jnp.sum(jnp.where(lo, 0.0, sq), axis=-1, keepdims=True)
                inv = lax.rsqrt(jnp.where(lo, s_lo, s_hi) * (1.0 / DH_DA) + EPS)
                a = a * inv * g
                a = (a * rc + pltpu.roll(a, LANES - half, 1) * rs1
                     + pltpu.roll(a, half, 1) * rs2)
                if which == 0:
                    a = a * (DH_DA ** -0.5 * LOG2E)
                out_ref[:, h * LANES:(h + 1) * LANES] = a.astype(BF16)

    ab = _dot(hm, wrest[:, 4 * DN_W:4 * DN_W + LANES])
    abt = ab.T
    n8 = 2 * N_DN

    def gates(da, db, alog, dtb):
        g = -jnp.exp(alog) * jax.nn.softplus(da + dtb)
        return g, jax.nn.sigmoid(db)

    gc, bc = gates(ab[:, 0:n8], ab[:, n8:2 * n8], alog_ref[0:1, 0:n8], dtb_ref[0:1, 0:n8])
    gcol_ref[...] = jnp.concatenate([gc, bc], axis=1)
    gr, br = gates(abt[0:n8], abt[n8:2 * n8], alog_ref[:, LANES:LANES + 1][0:n8],
                   dtb_ref[:, LANES:LANES + 1][0:n8])
    grow_ref[...] = jnp.concatenate([gr, br], axis=0)

    for c2 in range(3 * N_DN // 2):
        res = _dot(hn, wdn_ref[:, c2 * 2 * LANES:(c2 + 1) * 2 * LANES])
        pbuf[2 * c2] = res[:, 0:LANES]
        pbuf[2 * c2 + 1] = res[:, LANES:2 * LANES]
    r = DN_CONV // 2
    for c in range(3 * N_DN):
        cols = slice(c * LANES, (c + 1) * LANES)
        taps = [convw_ref[t:t + 1, cols] for t in range(DN_CONV)]
        pieces = []
        for r0, s in _row_groups(tm):
            for b in range(s):
                acc = pbuf[c, pl.ds(HALO - r + r0 + b, 8, stride=s), :] * taps[0]
                for t in range(1, DN_CONV):
                    acc = acc + pbuf[c, pl.ds(HALO - r + t + r0 + b, 8, stride=s), :] * taps[t]
                pieces.append(acc)
        y = _silu(jnp.concatenate(pieces, axis=0))
        if c < 2 * N_DN:
            y = y * lax.rsqrt(jnp.sum(y * y, axis=-1, keepdims=True) + EPS)
            if c < N_DN:
                y = y * (DK_DN ** -0.5)
        n = 0
        for r0, s in _row_groups(tm):
            for b in range(s):
                ybuf[c, pl.ds(r0 + b, 8, stride=s), :] = y[8 * n:8 * n + 8]
                n += 1
        dn_ref[:, cols] = ybuf[c].astype(BF16)

    z_ref[...] = _dot(hm, wrest[:, 0:DN_W]).astype(BF16)
    av_ref[...] = _dot(hm, wrest[:, 3 * DN_W:4 * DN_W]).astype(BF16)


def _inproj(x, S, prep, tm=512):
    T = x.shape[0]
    assert S % tm == 0 and T % S == 0
    nt = T // tm
    hb = tm // HALO
    nhb = T // HALO
    tps = S // tm
    kern = functools.partial(_inproj_kernel, tps, tm)
    out_shape = (
        jax.ShapeDtypeStruct((T, 3 * DN_W), BF16),
        jax.ShapeDtypeStruct((T, DN_W), BF16),
        jax.ShapeDtypeStruct((T, 16), F32),
        jax.ShapeDtypeStruct((16, T), F32),
        jax.ShapeDtypeStruct((T, DN_W), BF16),
        jax.ShapeDtypeStruct((T, DN_W), BF16),
        jax.ShapeDtypeStruct((T, DN_W), BF16),
    )
    tile = lambda w: pl.BlockSpec((tm, w), lambda i: (i, 0))
    in_specs = [
        pl.BlockSpec((HALO, D_MODEL), lambda i: (jnp.maximum(i * hb - 1, 0), 0)),
        tile(D_MODEL),
        pl.BlockSpec((HALO, D_MODEL), lambda i: (jnp.minimum((i + 1) * hb, nhb - 1), 0)),
        _const_spec((1, D_MODEL)),
        _const_spec(prep["w_dn"].shape),
        _const_spec(prep["w_rest"].shape),
        _const_spec(prep["dn_conv_w"].shape),
        _const_spec(prep["alog"].shape),
        _const_spec(prep["dtb"].shape),
        _const_spec(prep["qkg"].shape),
        pl.BlockSpec((tm, LANES), lambda i: (i % tps, 0)),
        pl.BlockSpec((tm, LANES), lambda i: (i % tps, 0)),
        pl.BlockSpec((tm, LANES), lambda i: (i % tps, 0)),
    ]
    out_specs = (tile(3 * DN_W), tile(DN_W), tile(16),
                 pl.BlockSpec((16, tm), lambda i: (0, i)),
                 tile(DN_W), tile(DN_W), tile(DN_W))
    rc, rs1, rs2 = prep["rope"][S]
    return pl.pallas_call(
        kern, grid=(nt,), in_specs=in_specs, out_specs=out_specs, out_shape=out_shape,
        scratch_shapes=[pltpu.VMEM((3 * N_DN, tm + 2 * HALO, LANES), F32),
                        pltpu.VMEM((3 * N_DN, tm, LANES), F32)],
        compiler_params=_cparams(("parallel",)), name="inproj",
    )(x, x, x, prep["ln1_g"], prep["w_dn"], prep["w_rest"], prep["dn_conv_w"],
      prep["alog"], prep["dtb"], prep["qkg"], rc, rs1, rs2)


DN_SUB = 256
DN_BT = 512


def _chunk_cumsum(x, axis, reverse):
    n = x.shape[axis]
    pos = lax.broadcasted_iota(jnp.int32, x.shape, axis) % CHUNK
    s = 1
    while s < CHUNK:
        if reverse:
            sh = pltpu.roll(x, n - s, axis)
            x = x + jnp.where(pos < CHUNK - s, sh, 0.0)
        else:
            sh = pltpu.roll(x, s, axis)
            x = x + jnp.where(pos >= s, sh, 0.0)
        s *= 2
    return x


def _dn_kernel(qkvf_ref, gcf_ref, grf_ref, qkvb_ref, gcb_ref, grb_ref,
               o_ref, s_ref, vn_ref):
    bt = DN_SUB
    nchunk = bt // CHUNK
    nstage = DN_BT // DN_SUB
    step = pl.program_id(1)
    nb = pl.num_programs(1)

    @pl.when(step == 0)
    def _():
        s_ref[...] = jnp.zeros_like(s_ref)
        o_ref[...] = jnp.zeros_like(o_ref)

    row = lax.broadcasted_iota(jnp.int32, (bt, bt), 0)
    col = lax.broadcasted_iota(jnp.int32, (bt, bt), 1)
    same = (row // CHUNK) == (col // CHUNK)
    eye = (row == col).astype(F32)

    def level(b):
        return ((row // (2 * b)) == (col // (2 * b))) & ((row // b) != (col // b))

    def run_stage(stage):
        probs = []
        dirs = ((qkvf_ref, gcf_ref, grf_ref, step), (qkvb_ref, gcb_ref, grb_ref, nb - 1 - step))
        for d, (qkv_ref, gc_ref, gr_ref, blk) in enumerate(dirs):
            rev = d == 1
            sub = nstage - 1 - stage if rev else stage
            tok = slice(sub * bt, (sub + 1) * bt)
            gcol = gc_ref[tok, :]
            grow = gr_ref[:, tok]
            cum_c = _chunk_cumsum(gcol, 0, rev)
            cum_r = _chunk_cumsum(grow, 1, rev)
            if rev:
                incl = same & (row <= col)
                strict = same & (row < col)
            else:
                incl = same & (row >= col)
                strict = same & (row > col)
            for h in range(N_DN):
                ci = d * N_DN + h
                probs.append(dict(
                    d=d, h=h, rev=rev, row0=blk * DN_BT + sub * bt, incl=incl, strict=strict,
                    gc_c=cum_c[:, ci:ci + 1], gc_r=cum_r[ci:ci + 1, :],
                    b_c=gcol[:, 2 * N_DN + ci:2 * N_DN + ci + 1],
                    q=qkv_ref[tok, h * LANES:(h + 1) * LANES],
                    k=qkv_ref[tok, DN_W + h * LANES:DN_W + (h + 1) * LANES],
                    v=qkv_ref[tok, 2 * DN_W + h * LANES:2 * DN_W + (h + 1) * LANES]))

        for p in probs:
            gam = jnp.exp(jnp.where(p["incl"], p["gc_c"] - p["gc_r"], -jnp.inf))
            kk = _dot_nt(p["k"], p["k"])
            qk = _dot_nt(p["q"], p["k"])
            a = jnp.where(p["strict"], p["b_c"] * kk * gam, 0.0)
            p["ab"] = a.astype(BF16)
            p["qkg"] = (qk * gam).astype(BF16)
            p["db"] = (eye - jnp.where(level(1), a, 0.0)).astype(BF16)

        b = 2
        while b < CHUNK:
            lm = level(b)
            for p in probs:
                p["x"] = jnp.where(lm, _dot(p["db"], p["ab"]), 0.0).astype(BF16)
            for p in probs:
                p["db"] = jnp.where(lm, (-_dot(p["x"], p["db"])).astype(BF16), p["db"])
            b *= 2

        for p in probs:
            e_c = jnp.exp(p["gc_c"])
            kf = p["k"].astype(F32)
            vb = (p["v"].astype(F32) * p["b_c"]).astype(BF16)
            ke = (kf * (e_c * p["b_c"])).astype(BF16)
            qd = (p["q"].astype(F32) * e_c).astype(BF16)
            uw = _dot(p["db"], jnp.concatenate([vb, ke], axis=1))
            p["u"] = uw[:, 0:LANES]
            w = uw[:, LANES:2 * LANES].astype(BF16)
            p["wq"], p["kd"], p["egl"] = [], [], []
            for c in range(nchunk):
                rows = slice(c * CHUNK, (c + 1) * CHUNK)
                last = c * CHUNK if p["rev"] else (c + 1) * CHUNK - 1
                gl = p["gc_c"][last:last + 1, :]
                p["wq"].append(jnp.concatenate([w[rows], qd[rows]], axis=0))
                p["kd"].append((kf[rows] * jnp.exp(gl - p["gc_c"][rows])).astype(BF16))
                p["egl"].append(jnp.exp(gl))
            p["state"] = s_ref[p["d"], p["h"]]
            p["o1"] = [None] * nchunk

        for t in range(nchunk):
            for i, p in enumerate(probs):
                c = nchunk - 1 - t if p["rev"] else t
                rows = slice(c * CHUNK, (c + 1) * CHUNK)
                pq = _dot(p["wq"][c], p["state"].astype(BF16))
                vnew = (p["u"][rows] - pq[0:CHUNK]).astype(BF16)
                p["o1"][c] = pq[CHUNK:2 * CHUNK]
                vn_ref[stage, i, rows, :] = vnew
                p["state"] = p["state"] * p["egl"][c] + lax.dot_general(
                    p["kd"][c], vnew, (((0,), (0,)), ((), ())), preferred_element_type=F32)

        for i, p in enumerate(probs):
            s_ref[p["d"], p["h"]] = p["state"]
            o = jnp.concatenate(p["o1"], axis=0) + _dot(p["qkg"], vn_ref[stage, i])
            rows = pl.ds(pl.multiple_of(p["row0"], bt), bt)
            o_ref[rows, p["h"] * LANES:(p["h"] + 1) * LANES] += o

    for stage in range(nstage):
        run_stage(stage)


def _deltanet(dnqkv, gcol, grow, B, S):
    T = dnqkv.shape[0]
    bt = DN_BT
    nb = S // bt
    fwd = lambda b, i: b * nb + i
    bwd = lambda b, i: b * nb + nb - 1 - i
    def specs(f):
        return [pl.BlockSpec((bt, 3 * DN_W), lambda b, i: (f(b, i), 0)),
                pl.BlockSpec((bt, 16), lambda b, i: (f(b, i), 0)),
                pl.BlockSpec((16, bt), lambda b, i: (0, f(b, i)))]
    return pl.pallas_call(
        _dn_kernel, grid=(B, nb),
        in_specs=specs(fwd) + specs(bwd),
        out_specs=pl.BlockSpec((S, DN_W), lambda b, i: (b, 0)),
        out_shape=jax.ShapeDtypeStruct((T, DN_W), F32),
        scratch_shapes=[pltpu.VMEM((2, N_DN, DK_DN, DK_DN), F32),
                        pltpu.VMEM((DN_BT // DN_SUB, 2 * N_DN, DN_SUB, DK_DN), BF16)],
        compiler_params=_cparams(("parallel", "arbitrary")), name="deltanet",
    )(dnqkv, gcol, grow, dnqkv, gcol, grow)


ATTN_TKB = 1024


def _attn_kernel(S, tq, tk, lam_init,
                 lam_ref, q_ref, k_ref, v_ref, g_ref, o_ref, s_scr, vaug):
    @pl.when(pl.program_id(2) == 0)
    def _():
        vaug[:, 0:LANES] = v_ref[...]
        vaug[:, LANES:2 * LANES] = jnp.ones((S, LANES), BF16)

    lf = lam_ref[...]
    lam = (jnp.exp(jnp.sum(lf[0:1] * lf[1:2], axis=-1, keepdims=True))
           - jnp.exp(jnp.sum(lf[2:3] * lf[3:4], axis=-1, keepdims=True)) + lam_init)
    q = q_ref[...]
    lane = lax.broadcasted_iota(jnp.int32, q.shape, 1)
    zero = jnp.zeros_like(q)
    qs = (jnp.where(lane < DH_DA, q, zero), jnp.where(lane < DH_DA, zero, q))

    m = []
    for c in range(2):
        mp = jnp.full((tq, LANES), -jnp.inf, F32)
        for i in range(S // tk):
            s = _dot_nt(qs[c], k_ref[i * tk:(i + 1) * tk, :])
            s_scr[c, :, i * tk:(i + 1) * tk] = s
            for j in range(tk // LANES):
                mp = jnp.maximum(mp, s[:, j * LANES:(j + 1) * LANES])
        m.append(jnp.max(mp, axis=-1, keepdims=True))

    outs = []
    for c in range(2):
        acc = jnp.zeros((tq, 2 * LANES), F32)
        for i in range(S // ATTN_TKB):
            sl = slice(i * ATTN_TKB, (i + 1) * ATTN_TKB)
            p = jnp.exp2(s_scr[c, :, sl] - m[c]).astype(BF16)
            acc = acc + _dot(p, vaug[sl, :])
        outs.append(acc)
    o1 = outs[0][:, 0:LANES] / outs[0][:, LANES:2 * LANES]
    o2 = outs[1][:, 0:LANES] / outs[1][:, LANES:2 * LANES]
    o_ref[...] = (_rms(o1 - lam * o2, g_ref[...]) * (1.0 - lam_init)).astype(BF16)


def _attn(aq, ak, av, da_lambda, subln_g, B, S, lam_init, tq=512, tk=512):
    T = aq.shape[0]
    nq = S // tq
    kern = functools.partial(_attn_kernel, S, tq, tk, lam_init)
    kv_spec = pl.BlockSpec((S, LANES), lambda b, h, i: (b, h))
    q_spec = pl.BlockSpec((tq, LANES), lambda b, h, i: (b * nq + i, h))
    return pl.pallas_call(
        kern, grid=(B, N_DA, nq),
        in_specs=[_const_spec(da_lambda.shape), q_spec, kv_spec, kv_spec,
                  _const_spec(subln_g.shape)],
        out_specs=q_spec,
        out_shape=jax.ShapeDtypeStruct((T, DN_W), BF16),
        scratch_shapes=[pltpu.VMEM((2, tq, S), F32), pltpu.VMEM((S, 2 * LANES), BF16)],
        compiler_params=_cparams(("parallel", "parallel", "arbitrary")), name="attn",
    )(da_lambda, aq, ak, av, subln_g)


FFN_FC = 256
FFN_NC = D_FF // FFN_FC


def _ffn_kernel(tiles_per_seq, tm, *refs):
    (xp_ref, x_ref, xn_ref, op_ref, o_ref, on_ref, zp_ref, z_ref, zn_ref,
     ap_ref, a_ref, an_ref, pe_ref,
     dng_ref, wout_ref, ln2_ref, wg_ref, wu_ref, cw_ref, cb_ref, wd_ref,
     pproj_ref, png_ref, pgn_ref, wpg_ref, out_ref, gbuf, ubuf, act_scr) = refs
    j = pl.program_id(0) % tiles_per_seq
    cat = lambda p, m, n: jnp.concatenate([p[...], m[...], n[...]], axis=0)

    o = cat(op_ref, o_ref, on_ref)
    z = cat(zp_ref, z_ref, zn_ref)
    dng = dng_ref[...]
    parts = []
    for h in range(N_DN):
        cols = slice(h * LANES, (h + 1) * LANES)
        parts.append((_rms(o[:, cols], dng) * _silu(z[:, cols].astype(F32))).astype(BF16))
    parts.append(cat(ap_ref, a_ref, an_ref))
    x1 = cat(xp_ref, x_ref, xn_ref) + _dot(jnp.concatenate(parts, axis=1), wout_ref[...])
    rowi = lax.broadcasted_iota(jnp.int32, (tm + 2 * HALO, 1), 0)
    pad = ((rowi < HALO) & (j == 0)) | ((rowi >= HALO + tm) & (j == tiles_per_seq - 1))
    x1 = jnp.where(pad, 0.0, x1)

    hn = _rms(x1, ln2_ref[...]).astype(BF16)
    hm = hn[HALO:HALO + tm]
    r = FFN_CONV // 2

    def matmuls(c, slot):
        gbuf[slot] = _dot(hn, wg_ref[c])
        ubuf[slot] = _dot(hm, wu_ref[c])

    def activate(c, slot):
        cw = cw_ref[c]
        gate = gbuf[slot, pl.ds(HALO - r, tm), :] * cw[0:1]
        for t in range(1, FFN_CONV):
            gate = gate + gbuf[slot, pl.ds(HALO - r + t, tm), :] * cw[t:t + 1]
        gate = gate + cb_ref[c]
        act = (_silu(gate) * ubuf[slot]).astype(BF16)
        act_scr[:, c * FFN_FC:(c + 1) * FFN_FC] = act

    matmuls(0, 0)
    for c in range(1, FFN_NC):
        matmuls(c, c % 2)
        activate(c - 1, (c - 1) % 2)
    activate(FFN_NC - 1, (FFN_NC - 1) % 2)
    x2 = x1[HALO:HALO + tm] + _dot(act_scr[...], wd_ref[...])
    e = _rms(_dot(pe_ref[...].astype(BF16), pproj_ref[...]), png_ref[...])
    gt = jax.nn.sigmoid(_dot(_rms(x2, pgn_ref[...]).astype(BF16), wpg_ref[...]))
    out_ref[...] = x2 + gt * e


def _ffn(x, osum, z, oda, pe, S, prep, tm=512):
    T = x.shape[0]
    hb = tm // HALO
    nhb = T // HALO
    tps = S // tm
    kern = functools.partial(_ffn_kernel, tps, tm)
    tile = lambda w: pl.BlockSpec((tm, w), lambda i: (i, 0))

    def with_halo(w):
        return [pl.BlockSpec((HALO, w), lambda i: (jnp.maximum(i * hb - 1, 0), 0)), tile(w),
                pl.BlockSpec((HALO, w), lambda i: (jnp.minimum((i + 1) * hb, nhb - 1), 0))]

    names = ("dn_norm_g", "w_out", "ln2_g", "w_gate", "w_upp", "ffn_conv_w", "ffn_conv_b", "w_down",
             "ple_proj", "ple_norm_g", "ple_gate_norm_g", "w_ple_gate")
    in_specs = (with_halo(D_MODEL) + with_halo(DN_W) + with_halo(DN_W) + with_halo(DN_W)
                + [tile(PLE_DIM)]
                + [pl.BlockSpec(prep[n].shape, functools.partial(lambda nd, i: (0,) * nd, prep[n].ndim),
                                pipeline_mode=pl.Buffered(1)) for n in names])
    return pl.pallas_call(
        kern, grid=(T // tm,), in_specs=in_specs, out_specs=tile(D_MODEL),
        out_shape=jax.ShapeDtypeStruct((T, D_MODEL), F32),
        scratch_shapes=[pltpu.VMEM((2, tm + 2 * HALO, FFN_FC), F32),
                        pltpu.VMEM((2, tm, FFN_FC), F32),
                        pltpu.VMEM((tm, D_FF), BF16)],
        compiler_params=_cparams(("parallel",)), name="ffn",
    )(x, x, x, osum, osum, osum, z, z, z, oda, oda, oda, pe, *[prep[n] for n in names])


def _rope_tables(S):
    half = ROT_DIM // 2
    inv_freq = ROPE_THETA ** (-jnp.arange(0, ROT_DIM, 2, dtype=F32) / ROT_DIM)
    ang = jnp.arange(S, dtype=F32)[:, None] * inv_freq[None, :]
    cos, sin = jnp.cos(ang), jnp.sin(ang)
    zeros = jnp.zeros((S, DH_DA - ROT_DIM), F32)
    zh = jnp.zeros((S, half), F32)
    c = jnp.concatenate([cos, cos, zeros + 1.0], axis=1)
    s1 = jnp.concatenate([-sin, zh, zeros], axis=1)
    s2 = jnp.concatenate([zh, sin, zeros], axis=1)
    rep = lambda a: jnp.concatenate([a, a], axis=1)
    return rep(c), rep(s1), rep(s2)


def _prepare(layer, seqs, ln1_g, w_in, dn_conv_w, dn_a_log, dn_dt_bias, dn_norm_g, da_qk_norm_g,
             da_lambda, da_subln_g, w_out, ln2_g, w_up, ffn_conv_w, ffn_conv_b, w_down,
             ple_proj, ple_norm_g, ple_gate_norm_g, w_ple_gate):
    row = lambda a: a[layer].reshape(1, -1).astype(F32)
    w = w_in[layer]
    o_z, o_ab, o_aq = 3 * DN_W, 4 * DN_W, 4 * DN_W + 16
    w_rest = jnp.concatenate(
        [w[:, o_z:o_ab], w[:, o_aq:o_aq + 3 * DN_W], w[:, o_ab:o_aq],
         jnp.zeros((D_MODEL, LANES - 16), w.dtype)], axis=1)
    n8 = 2 * N_DN
    flat8 = lambda a: a[layer].reshape(n8).astype(F32)
    both = lambda a: jnp.zeros((n8, 2 * LANES), F32).at[0, 0:n8].set(flat8(a)).at[:, LANES].set(flat8(a))
    chunked = lambda a: a.reshape(a.shape[0], FFN_NC, FFN_FC).transpose(1, 0, 2)
    wu = w_up[layer]
    return {
        "ln1_g": row(ln1_g),
        "w_dn": w[:, 0:3 * DN_W].astype(BF16),
        "w_rest": w_rest.astype(BF16),
        "dn_conv_w": dn_conv_w[layer].astype(F32),
        "alog": both(dn_a_log), "dtb": both(dn_dt_bias),
        "qkg": jnp.tile(da_qk_norm_g[layer].astype(F32), (1, 2)),
        "rope": {S: _rope_tables(S) for S in seqs},
        "dn_norm_g": row(dn_norm_g),
        "da_lambda": da_lambda[layer].astype(F32),
        "da_subln_g": row(da_subln_g),
        "w_out": w_out[layer].astype(BF16),
        "ln2_g": row(ln2_g),
        "w_gate": chunked(wu[:, 0:D_FF]).astype(BF16),
        "w_upp": chunked(wu[:, D_FF:]).astype(BF16),
        "ffn_conv_w": chunked(ffn_conv_w[layer].astype(F32)),
        "ffn_conv_b": ffn_conv_b[layer].astype(F32).reshape(FFN_NC, 1, FFN_FC),
        "w_down": w_down[layer].astype(BF16),
        "ple_proj": ple_proj[layer].astype(BF16),
        "ple_norm_g": row(ple_norm_g),
        "ple_gate_norm_g": row(ple_gate_norm_g),
        "w_ple_gate": w_ple_gate[layer].astype(BF16),
    }


def _encoder_layer(x, pe, layer, prep):
    B, S, _ = x.shape
    lam_init = 0.8 - 0.6 * math.exp(-0.3 * layer)
    xf = x.reshape(B * S, D_MODEL)
    dnqkv, z, gcol, grow, aq, ak, av = _inproj(xf, S, prep)
    osum = _deltanet(dnqkv, gcol, grow, B, S)
    oda = _attn(aq, ak, av, prep["da_lambda"], prep["da_subln_g"], B, S, lam_init)
    out = _ffn(xf, osum, z, oda, pe.reshape(B * S, PLE_DIM), S, prep)
    return out.reshape(B, S, D_MODEL)


def kernel(x_prompt, x_sample, p_prompt, p_sample, ln1_g, w_in, dn_conv_w, dn_a_log, dn_dt_bias, dn_norm_g, da_qk_norm_g, da_lambda, da_subln_g, w_out, ln2_g, w_up, ffn_conv_w, ffn_conv_b, w_down, ple_proj, ple_norm_g, ple_gate_norm_g, w_ple_gate):
    weights = (ln1_g, w_in, dn_conv_w, dn_a_log, dn_dt_bias, dn_norm_g, da_qk_norm_g, da_lambda,
               da_subln_g, w_out, ln2_g, w_up, ffn_conv_w, ffn_conv_b, w_down, ple_proj,
               ple_norm_g, ple_gate_norm_g, w_ple_gate)
    depth = w_in.shape[0]
    seqs = (x_prompt.shape[1], x_sample.shape[1])
    outs = []
    for x, p in ((x_prompt, p_prompt), (x_sample, p_sample)):
        for layer in range(depth):
            prep = _prepare(layer, seqs, *weights)
            x = _encoder_layer(x, p[layer], layer, prep)
        outs.append(x)
    return tuple(outs)
```

```python
import functools
import math

import jax
import jax.numpy as jnp
from jax import lax
from jax.experimental import pallas as pl
from jax.experimental.pallas import tpu as pltpu

D_MODEL = 1024
N_DN = 4
DK_DN = 128
DN_CONV = 5
CHUNK = 64
N_DA = 4
DH_DA = 64
ROT_DIM = DH_DA // 4
ROPE_THETA = 500000.0
D_FF = 2816
FFN_CONV = 3
PLE_DIM = 256
EPS = 1e-6
DN_W = N_DN * DK_DN
LANES = 128
HALO = 16
VMEM_LIMIT = 56 * 1024 * 1024

F32 = jnp.float32
BF16 = jnp.bfloat16
LOG2E = math.log2(math.e)


def _dot(a, b):
    return jnp.dot(a, b, preferred_element_type=F32)


def _dot_nt(a, b):
    return lax.dot_general(a, b, (((1,), (1,)), ((), ())), preferred_element_type=F32)


def _rms(x, g):
    return x * lax.rsqrt(jnp.mean(x * x, axis=-1, keepdims=True) + EPS) * g


def _silu(x):
    return x * jax.nn.sigmoid(x)


def _cparams(sem):
    return pltpu.CompilerParams(dimension_semantics=sem, vmem_limit_bytes=VMEM_LIMIT)


def _const_spec(shape):
    nd = len(shape)
    return pl.BlockSpec(shape, lambda *_: (0,) * nd)


def _row_groups(tm):
    groups = ((0, 36), (288, 28))
    assert tm == sum(8 * s for _, s in groups)
    return groups


def _inproj_kernel(tiles_per_seq, tm,
                   xp_ref, x_ref, xn_ref, ln_ref, wdn_ref, wrest_ref, convw_ref,
                   alog_ref, dtb_ref, qkg_ref, rc_ref, rs1_ref, rs2_ref,
                   dn_ref, z_ref, gcol_ref, grow_ref, aq_ref, ak_ref, av_ref, pbuf, ybuf):
    j = pl.program_id(0) % tiles_per_seq
    ln = ln_ref[...]
    xp = jnp.where(j == 0, 0.0, xp_ref[...])
    xn = jnp.where(j == tiles_per_seq - 1, 0.0, xn_ref[...])
    xx = jnp.concatenate([xp, x_ref[...], xn], axis=0)
    hn = _rms(xx, ln).astype(BF16)
    hm = hn[HALO:HALO + tm]

    wrest = wrest_ref
    lane = lax.broadcasted_iota(jnp.int32, (tm, LANES), 1)
    lo = lane < DH_DA
    rc, rs1, rs2 = rc_ref[...], rs1_ref[...], rs2_ref[...]
    half = ROT_DIM // 2
    for which, out_ref in ((0, aq_ref), (1, ak_ref)):
        base = (1 + which) * DN_W
        g = qkg_ref[which:which + 1, :]
        for h2 in range(N_DA // 2):
            pair = _dot(hm, wrest[:, base + h2 * 2 * LANES:base + (h2 + 1) * 2 * LANES])
            for h in (2 * h2, 2 * h2 + 1):
                a = pair[:, (h % 2) * LANES:(h % 2 + 1) * LANES]
                sq = a * a
                s_lo = jnp.sum(jnp.where(lo, sq, 0.0), axis=-1, keepdims=True)
                s_hi = jnp.sum(jnp.where(lo, 0.0, sq), axis=-1, keepdims=True)
                inv = lax.rsqrt(jnp.where(lo, s_lo, s_hi) * (1.0 / DH_DA) + EPS)
                a = a * inv * g
                a = (a * rc + pltpu.roll(a, LANES - half, 1) * rs1
                     + pltpu.roll(a, half, 1) * rs2)
                if which == 0:
                    a = a * (DH_DA ** -0.5 * LOG2E)
                out_ref[:, h * LANES:(h + 1) * LANES] = a.astype(BF16)

    ab = _dot(hm, wrest[:, 4 * DN_W:4 * DN_W + LANES])
    abt = ab.T
    n8 = 2 * N_DN

    def gates(da, db, alog, dtb):
        g = -jnp.exp(alog) * jax.nn.softplus(da + dtb)
        return g, jax.nn.sigmoid(db)

    gc, bc = gates(ab[:, 0:n8], ab[:, n8:2 * n8], alog_ref[0:1, 0:n8], dtb_ref[0:1, 0:n8])
    gcol_ref[...] = jnp.concatenate([gc, bc, jnp.zeros((tm, LANES - 2 * n8), F32)], axis=1)
    gr, br = gates(abt[0:n8], abt[n8:2 * n8], alog_ref[:, LANES:LANES + 1][0:n8],
                   dtb_ref[:, LANES:LANES + 1][0:n8])
    grow_ref[...] = jnp.concatenate([gr, br], axis=0)

    for c2 in range(3 * N_DN // 2):
        res = _dot(hn, wdn_ref[:, c2 * 2 * LANES:(c2 + 1) * 2 * LANES])
        pbuf[2 * c2] = res[:, 0:LANES]
        pbuf[2 * c2 + 1] = res[:, LANES:2 * LANES]
    r = DN_CONV // 2
    for c in range(3 * N_DN):
        cols = slice(c * LANES, (c + 1) * LANES)
        taps = [convw_ref[t:t + 1, cols] for t in range(DN_CONV)]
        pieces = []
        for r0, s in _row_groups(tm):
            for b in range(s):
                acc = pbuf[c, pl.ds(HALO - r + r0 + b, 8, stride=s), :] * taps[0]
                for t in range(1, DN_CONV):
                    acc = acc + pbuf[c, pl.ds(HALO - r + t + r0 + b, 8, stride=s), :] * taps[t]
                pieces.append(acc)
        y = _silu(jnp.concatenate(pieces, axis=0))
        if c < 2 * N_DN:
            y = y * lax.rsqrt(jnp.sum(y * y, axis=-1, keepdims=True) + EPS)
            if c < N_DN:
                y = y * (DK_DN ** -0.5)
        n = 0
        for r0, s in _row_groups(tm):
            for b in range(s):
                ybuf[c, pl.ds(r0 + b, 8, stride=s), :] = y[8 * n:8 * n + 8]
                n += 1
        dn_ref[:, cols] = ybuf[c].astype(BF16)

    z_ref[...] = _dot(hm, wrest[:, 0:DN_W]).astype(BF16)
    av_ref[...] = _dot(hm, wrest[:, 3 * DN_W:4 * DN_W]).astype(BF16)


def _inproj(x, S, prep, tm=512):
    T = x.shape[0]
    assert S % tm == 0 and T % S == 0
    nt = T // tm
    hb = tm // HALO
    nhb = T // HALO
    tps = S // tm
    kern = functools.partial(_inproj_kernel, tps, tm)
    out_shape = (
        jax.ShapeDtypeStruct((T, 3 * DN_W), BF16),
        jax.ShapeDtypeStruct((T, DN_W), BF16),
        jax.ShapeDtypeStruct((T, LANES), F32),
        jax.ShapeDtypeStruct((16, T), F32),
        jax.ShapeDtypeStruct((T, DN_W), BF16),
        jax.ShapeDtypeStruct((T, DN_W), BF16),
        jax.ShapeDtypeStruct((T, DN_W), BF16),
    )
    tile = lambda w: pl.BlockSpec((tm, w), lambda i: (i, 0))
    in_specs = [
        pl.BlockSpec((HALO, D_MODEL), lambda i: (jnp.maximum(i * hb - 1, 0), 0)),
        tile(D_MODEL),
        pl.BlockSpec((HALO, D_MODEL), lambda i: (jnp.minimum((i + 1) * hb, nhb - 1), 0)),
        _const_spec((1, D_MODEL)),
        _const_spec(prep["w_dn"].shape),
        _const_spec(prep["w_rest"].shape),
        _const_spec(prep["dn_conv_w"].shape),
        _const_spec(prep["alog"].shape),
        _const_spec(prep["dtb"].shape),
        _const_spec(prep["qkg"].shape),
        pl.BlockSpec((tm, LANES), lambda i: (i % tps, 0)),
        pl.BlockSpec((tm, LANES), lambda i: (i % tps, 0)),
        pl.BlockSpec((tm, LANES), lambda i: (i % tps, 0)),
    ]
    out_specs = (tile(3 * DN_W), tile(DN_W), tile(LANES),
                 pl.BlockSpec((16, tm), lambda i: (0, i)),
                 tile(DN_W), tile(DN_W), tile(DN_W))
    rc, rs1, rs2 = prep["rope"][S]
    return pl.pallas_call(
        kern, grid=(nt,), in_specs=in_specs, out_specs=out_specs, out_shape=out_shape,
        scratch_shapes=[pltpu.VMEM((3 * N_DN, tm + 2 * HALO, LANES), F32),
                        pltpu.VMEM((3 * N_DN, tm, LANES), F32)],
        compiler_params=_cparams(("parallel",)), name="inproj",
    )(x, x, x, prep["ln1_g"], prep["w_dn"], prep["w_rest"], prep["dn_conv_w"],
      prep["alog"], prep["dtb"], prep["qkg"], rc, rs1, rs2)


DN_SUB = 256
DN_BT = 512


def _chunk_cumsum(x, axis, reverse):
    n = x.shape[axis]
    pos = lax.broadcasted_iota(jnp.int32, x.shape, axis) % CHUNK
    s = 1
    while s < CHUNK:
        if reverse:
            sh = pltpu.roll(x, n - s, axis)
            x = x + jnp.where(pos < CHUNK - s, sh, 0.0)
        else:
            sh = pltpu.roll(x, s, axis)
            x = x + jnp.where(pos >= s, sh, 0.0)
        s *= 2
    return x


def _dn_kernel(qkvf_ref, gcf_ref, grf_ref, qkvb_ref, gcb_ref, grb_ref,
               o_ref, s_ref, vn_ref):
    bt = DN_SUB
    nchunk = bt // CHUNK
    nstage = DN_BT // DN_SUB
    step = pl.program_id(1)
    nb = pl.num_programs(1)

    @pl.when(step == 0)
    def _():
        s_ref[...] = jnp.zeros_like(s_ref)
        o_ref[...] = jnp.zeros_like(o_ref)

    row = lax.broadcasted_iota(jnp.int32, (bt, bt), 0)
    col = lax.broadcasted_iota(jnp.int32, (bt, bt), 1)
    same = (row // CHUNK) == (col // CHUNK)
    eye = (row == col).astype(F32)

    def level(b):
        return ((row // (2 * b)) == (col // (2 * b))) & ((row // b) != (col // b))

    def run_stage(stage):
        probs = []
        dirs = ((qkvf_ref, gcf_ref, grf_ref, step), (qkvb_ref, gcb_ref, grb_ref, nb - 1 - step))
        for d, (qkv_ref, gc_ref, gr_ref, blk) in enumerate(dirs):
            rev = d == 1
            sub = nstage - 1 - stage if rev else stage
            tok = slice(sub * bt, (sub + 1) * bt)
            gcol = gc_ref[tok, :]
            grow = gr_ref[:, tok]
            cum_c = _chunk_cumsum(gcol, 0, rev)
            cum_r = _chunk_cumsum(grow, 1, rev)
            if rev:
                incl = same & (row <= col)
                strict = same & (row < col)
            else:
                incl = same & (row >= col)
                strict = same & (row > col)
            for h in range(N_DN):
                ci = d * N_DN + h
                probs.append(dict(
                    d=d, h=h, rev=rev, row0=blk * DN_BT + sub * bt, incl=incl, strict=strict,
                    gc_c=cum_c[:, ci:ci + 1], gc_r=cum_r[ci:ci + 1, :],
                    b_c=gcol[:, 2 * N_DN + ci:2 * N_DN + ci + 1],
                    q=qkv_ref[tok, h * LANES:(h + 1) * LANES],
                    k=qkv_ref[tok, DN_W + h * LANES:DN_W + (h + 1) * LANES],
                    v=qkv_ref[tok, 2 * DN_W + h * LANES:2 * DN_W + (h + 1) * LANES]))

        for p in probs:
            gam = jnp.exp(jnp.where(p["incl"], p["gc_c"] - p["gc_r"], -jnp.inf))
            kk = _dot_nt(p["k"], p["k"])
            qk = _dot_nt(p["q"], p["k"])
            a = jnp.where(p["strict"], p["b_c"] * kk * gam, 0.0)
            p["ab"] = a.astype(BF16)
            p["qkg"] = (qk * gam).astype(BF16)
            p["db"] = (eye - jnp.where(level(1), a, 0.0)).astype(BF16)

        b = 2
        while b < CHUNK:
            lm = level(b)
            for p in probs:
                p["x"] = jnp.where(lm, _dot(p["db"], p["ab"]), 0.0).astype(BF16)
            for p in probs:
                p["db"] = jnp.where(lm, (-_dot(p["x"], p["db"])).astype(BF16), p["db"])
            b *= 2

        for p in probs:
            e_c = jnp.exp(p["gc_c"])
            kf = p["k"].astype(F32)
            vb = (p["v"].astype(F32) * p["b_c"]).astype(BF16)
            ke = (kf * (e_c * p["b_c"])).astype(BF16)
            qd = (p["q"].astype(F32) * e_c).astype(BF16)
            uw = _dot(p["db"], jnp.concatenate([vb, ke], axis=1))
            p["u"] = uw[:, 0:LANES]
            w = uw[:, LANES:2 * LANES].astype(BF16)
            p["wq"], p["kd"], p["egl"] = [], [], []
            for c in range(nchunk):
                rows = slice(c * CHUNK, (c + 1) * CHUNK)
                last = c * CHUNK if p["rev"] else (c + 1) * CHUNK - 1
                gl = p["gc_c"][last:last + 1, :]
                p["wq"].append(jnp.concatenate([w[rows], qd[rows]], axis=0))
                p["kd"].append((kf[rows] * jnp.exp(gl - p["gc_c"][rows])).astype(BF16))
                p["egl"].append(jnp.exp(gl))
            p["state"] = s_ref[p["d"], p["h"]]
            p["o1"] = [None] * nchunk

        for t in range(nchunk):
            for i, p in enumerate(probs):
                c = nchunk - 1 - t if p["rev"] else t
                rows = slice(c * CHUNK, (c + 1) * CHUNK)
                pq = _dot(p["wq"][c], p["state"].astype(BF16))
                vnew = (p["u"][rows] - pq[0:CHUNK]).astype(BF16)
                p["o1"][c] = pq[CHUNK:2 * CHUNK]
                vn_ref[stage, i, rows, :] = vnew
                p["state"] = p["state"] * p["egl"][c] + lax.dot_general(
                    p["kd"][c], vnew, (((0,), (0,)), ((), ())), preferred_element_type=F32)

        for i, p in enumerate(probs):
            s_ref[p["d"], p["h"]] = p["state"]
            o = jnp.concatenate(p["o1"], axis=0) + _dot(p["qkg"], vn_ref[stage, i])
            rows = pl.ds(pl.multiple_of(p["row0"], bt), bt)
            o_ref[rows, p["h"] * LANES:(p["h"] + 1) * LANES] += o

    for stage in range(nstage):
        run_stage(stage)


def _deltanet(dnqkv, gcol, grow, B, S):
    T = dnqkv.shape[0]
    bt = DN_BT
    nb = S // bt
    fwd = lambda b, i: b * nb + i
    bwd = lambda b, i: b * nb + nb - 1 - i
    def specs(f):
        return [pl.BlockSpec((bt, 3 * DN_W), lambda b, i: (f(b, i), 0)),
                pl.BlockSpec((bt, LANES), lambda b, i: (f(b, i), 0)),
                pl.BlockSpec((16, bt), lambda b, i: (0, f(b, i)))]
    return pl.pallas_call(
        _dn_kernel, grid=(B, nb),
        in_specs=specs(fwd) + specs(bwd),
        out_specs=pl.BlockSpec((S, DN_W), lambda b, i: (b, 0)),
        out_shape=jax.ShapeDtypeStruct((T, DN_W), F32),
        scratch_shapes=[pltpu.VMEM((2, N_DN, DK_DN, DK_DN), F32),
                        pltpu.VMEM((DN_BT // DN_SUB, 2 * N_DN, DN_SUB, DK_DN), BF16)],
        compiler_params=_cparams(("parallel", "arbitrary")), name="deltanet",
    )(dnqkv, gcol, grow, dnqkv, gcol, grow)


ATTN_TKB = 1024


def _attn_kernel(S, tq, tk, lam_init,
                 lam_ref, q_ref, k_ref, v_ref, g_ref, o_ref, s_scr, vaug):
    @pl.when(pl.program_id(2) == 0)
    def _():
        vaug[:, 0:LANES] = v_ref[...]
        vaug[:, LANES:2 * LANES] = jnp.ones((S, LANES), BF16)

    lf = lam_ref[...]
    lam = (jnp.exp(jnp.sum(lf[0:1] * lf[1:2], axis=-1, keepdims=True))
           - jnp.exp(jnp.sum(lf[2:3] * lf[3:4], axis=-1, keepdims=True)) + lam_init)
    q = q_ref[...]
    lane = lax.broadcasted_iota(jnp.int32, q.shape, 1)
    zero = jnp.zeros_like(q)
    qs = (jnp.where(lane < DH_DA, q, zero), jnp.where(lane < DH_DA, zero, q))

    m = []
    for c in range(2):
        mp = jnp.full((tq, LANES), -jnp.inf, F32)
        for i in range(S // tk):
            s = _dot_nt(qs[c], k_ref[i * tk:(i + 1) * tk, :])
            s_scr[c, :, i * tk:(i + 1) * tk] = s
            for j in range(tk // LANES):
                mp = jnp.maximum(mp, s[:, j * LANES:(j + 1) * LANES])
        m.append(jnp.max(mp, axis=-1, keepdims=True))

    outs = []
    for c in range(2):
        acc = jnp.zeros((tq, 2 * LANES), F32)
        for i in range(S // ATTN_TKB):
            sl = slice(i * ATTN_TKB, (i + 1) * ATTN_TKB)
            p = jnp.exp2(s_scr[c, :, sl] - m[c]).astype(BF16)
            acc = acc + _dot(p, vaug[sl, :])
        outs.append(acc)
    o1 = outs[0][:, 0:LANES] / outs[0][:, LANES:2 * LANES]
    o2 = outs[1][:, 0:LANES] / outs[1][:, LANES:2 * LANES]
    o_ref[...] = (_rms(o1 - lam * o2, g_ref[...]) * (1.0 - lam_init)).astype(BF16)


def _attn(aq, ak, av, da_lambda, subln_g, B, S, lam_init, tq=512, tk=512):
    T = aq.shape[0]
    nq = S // tq
    kern = functools.partial(_attn_kernel, S, tq, tk, lam_init)
    kv_spec = pl.BlockSpec((S, LANES), lambda b, h, i: (b, h))
    q_spec = pl.BlockSpec((tq, LANES), lambda b, h, i: (b * nq + i, h))
    return pl.pallas_call(
        kern, grid=(B, N_DA, nq),
        in_specs=[_const_spec(da_lambda.shape), q_spec, kv_spec, kv_spec,
                  _const_spec(subln_g.shape)],
        out_specs=q_spec,
        out_shape=jax.ShapeDtypeStruct((T, DN_W), BF16),
        scratch_shapes=[pltpu.VMEM((2, tq, S), F32), pltpu.VMEM((S, 2 * LANES), BF16)],
        compiler_params=_cparams(("parallel", "parallel", "arbitrary")), name="attn",
    )(da_lambda, aq, ak, av, subln_g)


FFN_FC = 256
FFN_NC = D_FF // FFN_FC


def _ffn_kernel(tiles_per_seq, tm, *refs):
    (xp_ref, x_ref, xn_ref, op_ref, o_ref, on_ref, zp_ref, z_ref, zn_ref,
     ap_ref, a_ref, an_ref, pe_ref,
     dng_ref, wout_ref, ln2_ref, wg_ref, wu_ref, cw_ref, cb_ref, wd_ref,
     pproj_ref, png_ref, pgn_ref, wpg_ref, out_ref, gbuf, ubuf, act_scr) = refs
    j = pl.program_id(0) % tiles_per_seq
    cat = lambda p, m, n: jnp.concatenate([p[...], m[...], n[...]], axis=0)

    o = cat(op_ref, o_ref, on_ref)
    z = cat(zp_ref, z_ref, zn_ref)
    dng = dng_ref[...]
    parts = []
    for h in range(N_DN):
        cols = slice(h * LANES, (h + 1) * LANES)
        parts.append((_rms(o[:, cols], dng) * _silu(z[:, cols].astype(F32))).astype(BF16))
    parts.append(cat(ap_ref, a_ref, an_ref))
    x1 = cat(xp_ref, x_ref, xn_ref) + _dot(jnp.concatenate(parts, axis=1), wout_ref[...])
    rowi = lax.broadcasted_iota(jnp.int32, (tm + 2 * HALO, 1), 0)
    pad = ((rowi < HALO) & (j == 0)) | ((rowi >= HALO + tm) & (j == tiles_per_seq - 1))
    x1 = jnp.where(pad, 0.0, x1)

    hn = _rms(x1, ln2_ref[...]).astype(BF16)
    hm = hn[HALO:HALO + tm]
    r = FFN_CONV // 2

    def matmuls(c, slot):
        gbuf[slot] = _dot(hn, wg_ref[c])
        ubuf[slot] = _dot(hm, wu_ref[c])

    def activate(c, slot):
        cw = cw_ref[c]
        gate = gbuf[slot, pl.ds(HALO - r, tm), :] * cw[0:1]
        for t in range(1, FFN_CONV):
            gate = gate + gbuf[slot, pl.ds(HALO - r + t, tm), :] * cw[t:t + 1]
        gate = gate + cb_ref[c]
        act = (_silu(gate) * ubuf[slot]).astype(BF16)
        act_scr[:, c * FFN_FC:(c + 1) * FFN_FC] = act

    matmuls(0, 0)
    for c in range(1, FFN_NC):
        matmuls(c, c % 2)
        activate(c - 1, (c - 1) % 2)
    activate(FFN_NC - 1, (FFN_NC - 1) % 2)
    x2 = x1[HALO:HALO + tm] + _dot(act_scr[...], wd_ref[...])
    e = _rms(_dot(pe_ref[...].astype(BF16), pproj_ref[...]), png_ref[...])
    gt = jax.nn.sigmoid(_dot(_rms(x2, pgn_ref[...]).astype(BF16), wpg_ref[...]))
    out_ref[...] = x2 + gt * e


def _ffn(x, osum, z, oda, pe, S, prep, tm=512):
    T = x.shape[0]
    hb = tm // HALO
    nhb = T // HALO
    tps = S // tm
    kern = functools.partial(_ffn_kernel, tps, tm)
    tile = lambda w: pl.BlockSpec((tm, w), lambda i: (i, 0))

    def with_halo(w):
        return [pl.BlockSpec((HALO, w), lambda i: (jnp.maximum(i * hb - 1, 0), 0)), tile(w),
                pl.BlockSpec((HALO, w), lambda i: (jnp.minimum((i + 1) * hb, nhb - 1), 0))]

    names = ("dn_norm_g", "w_out", "ln2_g", "w_gate", "w_upp", "ffn_conv_w", "ffn_conv_b", "w_down",
             "ple_proj", "ple_norm_g", "ple_gate_norm_g", "w_ple_gate")
    in_specs = (with_halo(D_MODEL) + with_halo(DN_W) + with_halo(DN_W) + with_halo(DN_W)
                + [tile(PLE_DIM)]
                + [pl.BlockSpec(prep[n].shape, functools.partial(lambda nd, i: (0,) * nd, prep[n].ndim),
                                pipeline_mode=pl.Buffered(1)) for n in names])
    return pl.pallas_call(
        kern, grid=(T // tm,), in_specs=in_specs, out_specs=tile(D_MODEL),
        out_shape=jax.ShapeDtypeStruct((T, D_MODEL), F32),
        scratch_shapes=[pltpu.VMEM((2, tm + 2 * HALO, FFN_FC), F32),
                        pltpu.VMEM((2, tm, FFN_FC), F32),
                        pltpu.VMEM((tm, D_FF), BF16)],
        compiler_params=_cparams(("parallel",)), name="ffn",
    )(x, x, x, osum, osum, osum, z, z, z, oda, oda, oda, pe, *[prep[n] for n in names])


def _rope_tables(S):
    half = ROT_DIM // 2
    inv_freq = ROPE_THETA ** (-jnp.arange(0, ROT_DIM, 2, dtype=F32) / ROT_DIM)
    ang = jnp.arange(S, dtype=F32)[:, None] * inv_freq[None, :]
    cos, sin = jnp.cos(ang), jnp.sin(ang)
    zeros = jnp.zeros((S, DH_DA - ROT_DIM), F32)
    zh = jnp.zeros((S, half), F32)
    c = jnp.concatenate([cos, cos, zeros + 1.0], axis=1)
    s1 = jnp.concatenate([-sin, zh, zeros], axis=1)
    s2 = jnp.concatenate([zh, sin, zeros], axis=1)
    rep = lambda a: jnp.concatenate([a, a], axis=1)
    return rep(c), rep(s1), rep(s2)


def _prepare(layer, seqs, ln1_g, w_in, dn_conv_w, dn_a_log, dn_dt_bias, dn_norm_g, da_qk_norm_g,
             da_lambda, da_subln_g, w_out, ln2_g, w_up, ffn_conv_w, ffn_conv_b, w_down,
             ple_proj, ple_norm_g, ple_gate_norm_g, w_ple_gate):
    row = lambda a: a[layer].reshape(1, -1).astype(F32)
    w = w_in[layer]
    o_z, o_ab, o_aq = 3 * DN_W, 4 * DN_W, 4 * DN_W + 16
    w_rest = jnp.concatenate(
        [w[:, o_z:o_ab], w[:, o_aq:o_aq + 3 * DN_W], w[:, o_ab:o_aq],
         jnp.zeros((D_MODEL, LANES - 16), w.dtype)], axis=1)
    n8 = 2 * N_DN
    flat8 = lambda a: a[layer].reshape(n8).astype(F32)
    both = lambda a: jnp.zeros((n8, 2 * LANES), F32).at[0, 0:n8].set(flat8(a)).at[:, LANES].set(flat8(a))
    chunked = lambda a: a.reshape(a.shape[0], FFN_NC, FFN_FC).transpose(1, 0, 2)
    wu = w_up[layer]
    return {
        "ln1_g": row(ln1_g),
        "w_dn": w[:, 0:3 * DN_W].astype(BF16),
        "w_rest": w_rest.astype(BF16),
        "dn_conv_w": dn_conv_w[layer].astype(F32),
        "alog": both(dn_a_log), "dtb": both(dn_dt_bias),
        "qkg": jnp.tile(da_qk_norm_g[layer].astype(F32), (1, 2)),
        "rope": {S: _rope_tables(S) for S in seqs},
        "dn_norm_g": row(dn_norm_g),
        "da_lambda": da_lambda[layer].astype(F32),
        "da_subln_g": row(da_subln_g),
        "w_out": w_out[layer].astype(BF16),
        "ln2_g": row(ln2_g),
        "w_gate": chunked(wu[:, 0:D_FF]).astype(BF16),
        "w_upp": chunked(wu[:, D_FF:]).astype(BF16),
        "ffn_conv_w": chunked(ffn_conv_w[layer].astype(F32)),
        "ffn_conv_b": ffn_conv_b[layer].astype(F32).reshape(FFN_NC, 1, FFN_FC),
        "w_down": w_down[layer].astype(BF16),
        "ple_proj": ple_proj[layer].astype(BF16),
        "ple_norm_g": row(ple_norm_g),
        "ple_gate_norm_g": row(ple_gate_norm_g),
        "w_ple_gate": w_ple_gate[layer].astype(BF16),
    }


def _encoder_layer(x, pe, layer, prep):
    B, S, _ = x.shape
    lam_init = 0.8 - 0.6 * math.exp(-0.3 * layer)
    xf = x.reshape(B * S, D_MODEL)
    dnqkv, z, gcol, grow, aq, ak, av = _inproj(xf, S, prep)
    osum = _deltanet(dnqkv, gcol, grow, B, S)
    oda = _attn(aq, ak, av, prep["da_lambda"], prep["da_subln_g"], B, S, lam_init)
    out = _ffn(xf, osum, z, oda, pe.reshape(B * S, PLE_DIM), S, prep)
    return out.reshape(B, S, D_MODEL)


def kernel(x_prompt, x_sample, p_prompt, p_sample, ln1_g, w_in, dn_conv_w, dn_a_log, dn_dt_bias, dn_norm_g, da_qk_norm_g, da_lambda, da_subln_g, w_out, ln2_g, w_up, ffn_conv_w, ffn_conv_b, w_down, ple_proj, ple_norm_g, ple_gate_norm_g, w_ple_gate):
    weights = (ln1_g, w_in, dn_conv_w, dn_a_log, dn_dt_bias, dn_norm_g, da_qk_norm_g, da_lambda,
               da_subln_g, w_out, ln2_g, w_up, ffn_conv_w, ffn_conv_b, w_down, ple_proj,
               ple_norm_g, ple_gate_norm_g, w_ple_gate)
    depth = w_in.shape[0]
    seqs = (x_prompt.shape[1], x_sample.shape[1])
    outs = []
    for x, p in ((x_prompt, p_prompt), (x_sample, p_sample)):
        for layer in range(depth):
            prep = _prepare(layer, seqs, *weights)
            x = _encoder_layer(x, p[layer], layer, prep)
        outs.append(x)
    return tuple(outs)
```

```python
import functools
import math

import jax
import jax.numpy as jnp
from jax import lax
from jax.experimental import pallas as pl
from jax.experimental.pallas import tpu as pltpu

D_MODEL = 1024
N_DN = 4
DK_DN = 128
DN_CONV = 5
CHUNK = 64
N_DA = 4
DH_DA = 64
ROT_DIM = DH_DA // 4
ROPE_THETA = 500000.0
D_FF = 2816
FFN_CONV = 3
PLE_DIM = 256
EPS = 1e-6
DN_W = N_DN * DK_DN
LANES = 128
HALO = 16
VMEM_LIMIT = 56 * 1024 * 1024

F32 = jnp.float32
BF16 = jnp.bfloat16
LOG2E = math.log2(math.e)


def _dot(a, b):
    return jnp.dot(a, b, preferred_element_type=F32)


def _dot_nt(a, b):
    return lax.dot_general(a, b, (((1,), (1,)), ((), ())), preferred_element_type=F32)


def _rms(x, g):
    return x * lax.rsqrt(jnp.mean(x * x, axis=-1, keepdims=True) + EPS) * g


def _silu(x):
    return x * jax.nn.sigmoid(x)


def _cparams(sem):
    return pltpu.CompilerParams(dimension_semantics=sem, vmem_limit_bytes=VMEM_LIMIT)


def _const_spec(shape):
    nd = len(shape)
    return pl.BlockSpec(shape, lambda *_: (0,) * nd)


def _row_groups(tm):
    groups = ((0, 36), (288, 28))
    assert tm == sum(8 * s for _, s in groups)
    return groups


def _inproj_kernel(tiles_per_seq, tm,
                   xp_ref, x_ref, xn_ref, ln_ref, wdn_ref, wrest_ref, convw_ref,
                   alog_ref, dtb_ref, qkg_ref, rc_ref, rs1_ref, rs2_ref,
                   dn_ref, z_ref, gcol_ref, grow_ref, aq_ref, ak_ref, av_ref, pbuf, ybuf):
    j = pl.program_id(0) % tiles_per_seq
    ln = ln_ref[...]
    xp = jnp.where(j == 0, 0.0, xp_ref[...])
    xn = jnp.where(j == tiles_per_seq - 1, 0.0, xn_ref[...])
    xx = jnp.concatenate([xp, x_ref[...], xn], axis=0)
    hn = _rms(xx, ln).astype(BF16)
    hm = hn[HALO:HALO + tm]

    wrest = wrest_ref
    lane = lax.broadcasted_iota(jnp.int32, (tm, LANES), 1)
    lo = lane < DH_DA
    rc, rs1, rs2 = rc_ref[...], rs1_ref[...], rs2_ref[...]
    half = ROT_DIM // 2
    for which, out_ref in ((0, aq_ref), (1, ak_ref)):
        base = (1 + which) * DN_W
        g = qkg_ref[which:which + 1, :]
        for h2 in range(N_DA // 2):
            pair = _dot(hm, wrest[:, base + h2 * 2 * LANES:base + (h2 + 1) * 2 * LANES])
            for h in (2 * h2, 2 * h2 + 1):
                a = pair[:, (h % 2) * LANES:(h % 2 + 1) * LANES]
                sq = a * a
                s_lo = jnp.sum(jnp.where(lo, sq, 0.0), axis=-1, keepdims=True)
                s_hi = jnp.sum(jnp.where(lo, 0.0, sq), axis=-1, keepdims=True)
                inv = lax.rsqrt(jnp.where(lo, s_lo, s_hi) * (1.0 / DH_DA) + EPS)
                a = a * inv * g
                a = (a * rc + pltpu.roll(a, LANES - half, 1) * rs1
                     + pltpu.roll(a, half, 1) * rs2)
                if which == 0:
                    a = a * (DH_DA ** -0.5 * LOG2E)
                out_ref[:, h * LANES:(h + 1) * LANES] = a.astype(BF16)

    ab = _dot(hm, wrest[:, 4 * DN_W:4 * DN_W + LANES])
    abt = ab.T
    n8 = 2 * N_DN

    def gates(da, db, alog, dtb):
        g = -jnp.exp(alog) * jax.nn.softplus(da + dtb)
        return g, jax.nn.sigmoid(db)

    gc, bc = gates(ab[:, 0:n8], ab[:, n8:2 * n8], alog_ref[0:1, 0:n8], dtb_ref[0:1, 0:n8])
    gcol_ref[...] = jnp.concatenate([gc, bc], axis=1)
    gr, br = gates(abt[0:n8], abt[n8:2 * n8], alog_ref[:, LANES:LANES + 1][0:n8],
                   dtb_ref[:, LANES:LANES + 1][0:n8])
    grow_ref[...] = jnp.concatenate([gr, br], axis=0)

    for c2 in range(3 * N_DN // 2):
        res = _dot(hn, wdn_ref[:, c2 * 2 * LANES:(c2 + 1) * 2 * LANES])
        pbuf[2 * c2] = res[:, 0:LANES]
        pbuf[2 * c2 + 1] = res[:, LANES:2 * LANES]
    r = DN_CONV // 2
    for c in range(3 * N_DN):
        cols = slice(c * LANES, (c + 1) * LANES)
        taps = [convw_ref[t:t + 1, cols] for t in range(DN_CONV)]
        pieces = []
        for r0, s in _row_groups(tm):
            for b in range(s):
                acc = pbuf[c, pl.ds(HALO - r + r0 + b, 8, stride=s), :] * taps[0]
                for t in range(1, DN_CONV):
                    acc = acc + pbuf[c, pl.ds(HALO - r + t + r0 + b, 8, stride=s), :] * taps[t]
                pieces.append(acc)
        y = _silu(jnp.concatenate(pieces, axis=0))
        if c < 2 * N_DN:
            y = y * lax.rsqrt(jnp.sum(y * y, axis=-1, keepdims=True) + EPS)
            if c < N_DN:
                y = y * (DK_DN ** -0.5)
        n = 0
        for r0, s in _row_groups(tm):
            for b in range(s):
                ybuf[c, pl.ds(r0 + b, 8, stride=s), :] = y[8 * n:8 * n + 8]
                n += 1
        dn_ref[:, cols] = ybuf[c].astype(BF16)

    z_ref[...] = _dot(hm, wrest[:, 0:DN_W]).astype(BF16)
    av_ref[...] = _dot(hm, wrest[:, 3 * DN_W:4 * DN_W]).astype(BF16)


def _inproj(x, S, prep, tm=512):
    T = x.shape[0]
    assert S % tm == 0 and T % S == 0
    nt = T // tm
    hb = tm // HALO
    nhb = T // HALO
    tps = S // tm
    kern = functools.partial(_inproj_kernel, tps, tm)
    out_shape = (
        jax.ShapeDtypeStruct((T, 3 * DN_W), BF16),
        jax.ShapeDtypeStruct((T, DN_W), BF16),
        jax.ShapeDtypeStruct((T, 16), F32),
        jax.ShapeDtypeStruct((16, T), F32),
        jax.ShapeDtypeStruct((T, DN_W), BF16),
        jax.ShapeDtypeStruct((T, DN_W), BF16),
        jax.ShapeDtypeStruct((T, DN_W), BF16),
    )
    tile = lambda w: pl.BlockSpec((tm, w), lambda i: (i, 0))
    in_specs = [
        pl.BlockSpec((HALO, D_MODEL), lambda i: (jnp.maximum(i * hb - 1, 0), 0)),
        tile(D_MODEL),
        pl.BlockSpec((HALO, D_MODEL), lambda i: (jnp.minimum((i + 1) * hb, nhb - 1), 0)),
        _const_spec((1, D_MODEL)),
        _const_spec(prep["w_dn"].shape),
        _const_spec(prep["w_rest"].shape),
        _const_spec(prep["dn_conv_w"].shape),
        _const_spec(prep["alog"].shape),
        _const_spec(prep["dtb"].shape),
        _const_spec(prep["qkg"].shape),
        pl.BlockSpec((tm, LANES), lambda i: (i % tps, 0)),
        pl.BlockSpec((tm, LANES), lambda i: (i % tps, 0)),
        pl.BlockSpec((tm, LANES), lambda i: (i % tps, 0)),
    ]
    out_specs = (tile(3 * DN_W), tile(DN_W), tile(16),
                 pl.BlockSpec((16, tm), lambda i: (0, i)),
                 tile(DN_W), tile(DN_W), tile(DN_W))
    rc, rs1, rs2 = prep["rope"][S]
    return pl.pallas_call(
        kern, grid=(nt,), in_specs=in_specs, out_specs=out_specs, out_shape=out_shape,
        scratch_shapes=[pltpu.VMEM((3 * N_DN, tm + 2 * HALO, LANES), F32),
                        pltpu.VMEM((3 * N_DN, tm, LANES), F32)],
        compiler_params=_cparams(("parallel",)), name="inproj",
    )(x, x, x, prep["ln1_g"], prep["w_dn"], prep["w_rest"], prep["dn_conv_w"],
      prep["alog"], prep["dtb"], prep["qkg"], rc, rs1, rs2)


DN_SUB = 256
DN_BT = 512


def _chunk_cumsum(x, axis, reverse):
    n = x.shape[axis]
    pos = lax.broadcasted_iota(jnp.int32, x.shape, axis) % CHUNK
    s = 1
    while s < CHUNK:
        if reverse:
            sh = pltpu.roll(x, n - s, axis)
            x = x + jnp.where(pos < CHUNK - s, sh, 0.0)
        else:
            sh = pltpu.roll(x, s, axis)
            x = x + jnp.where(pos >= s, sh, 0.0)
        s *= 2
    return x


def _dn_kernel(qkvf_ref, gcf_ref, grf_ref, qkvb_ref, gcb_ref, grb_ref,
               o_ref, s_ref, vn_ref):
    bt = DN_SUB
    nchunk = bt // CHUNK
    nstage = DN_BT // DN_SUB
    step = pl.program_id(1)
    nb = pl.num_programs(1)

    @pl.when(step == 0)
    def _():
        s_ref[...] = jnp.zeros_like(s_ref)
        o_ref[...] = jnp.zeros_like(o_ref)

    row = lax.broadcasted_iota(jnp.int32, (bt, bt), 0)
    col = lax.broadcasted_iota(jnp.int32, (bt, bt), 1)
    same = (row // CHUNK) == (col // CHUNK)
    eye = (row == col).astype(F32)

    def level(b):
        return ((row // (2 * b)) == (col // (2 * b))) & ((row // b) != (col // b))

    def run_stage(stage):
        probs = []
        dirs = ((qkvf_ref, gcf_ref, grf_ref, step), (qkvb_ref, gcb_ref, grb_ref, nb - 1 - step))
        for d, (qkv_ref, gc_ref, gr_ref, blk) in enumerate(dirs):
            rev = d == 1
            sub = nstage - 1 - stage if rev else stage
            tok = slice(sub * bt, (sub + 1) * bt)
            gcol = gc_ref[tok, :]
            grow = gr_ref[:, tok]
            cum_c = _chunk_cumsum(gcol, 0, rev)
            cum_r = _chunk_cumsum(grow, 1, rev)
            if rev:
                incl = same & (row <= col)
                strict = same & (row < col)
            else:
                incl = same & (row >= col)
                strict = same & (row > col)
            for h in range(N_DN):
                ci = d * N_DN + h
                probs.append(dict(
                    d=d, h=h, rev=rev, row0=blk * DN_BT + sub * bt, incl=incl, strict=strict,
                    gc_c=cum_c[:, ci:ci + 1], gc_r=cum_r[ci:ci + 1, :],
                    b_c=gcol[:, 2 * N_DN + ci:2 * N_DN + ci + 1],
                    q=qkv_ref[tok, h * LANES:(h + 1) * LANES],
                    k=qkv_ref[tok, DN_W + h * LANES:DN_W + (h + 1) * LANES],
                    v=qkv_ref[tok, 2 * DN_W + h * LANES:2 * DN_W + (h + 1) * LANES]))

        for p in probs:
            gam = jnp.exp(jnp.where(p["incl"], p["gc_c"] - p["gc_r"], -jnp.inf))
            kk = _dot_nt(p["k"], p["k"])
            qk = _dot_nt(p["q"], p["k"])
            a = jnp.where(p["strict"], p["b_c"] * kk * gam, 0.0)
            p["ab"] = a.astype(BF16)
            p["qkg"] = (qk * gam).astype(BF16)
            p["db"] = (eye - jnp.where(level(1), a, 0.0)).astype(BF16)

        b = 2
        while b < CHUNK:
            lm = level(b)
            for p in probs:
                p["x"] = jnp.where(lm, _dot(p["db"], p["ab"]), 0.0).astype(BF16)
            for p in probs:
                p["db"] = jnp.where(lm, (-_dot(p["x"], p["db"])).astype(BF16), p["db"])
            b *= 2

        for p in probs:
            e_c = jnp.exp(p["gc_c"])
            kf = p["k"].astype(F32)
            vb = (p["v"].astype(F32) * p["b_c"]).astype(BF16)
            ke = (kf * (e_c * p["b_c"])).astype(BF16)
            qd = (p["q"].astype(F32) * e_c).astype(BF16)
            uw = _dot(p["db"], jnp.concatenate([vb, ke], axis=1))
            p["u"] = uw[:, 0:LANES]
            w = uw[:, LANES:2 * LANES].astype(BF16)
            p["wq"], p["kd"], p["egl"] = [], [], []
            for c in range(nchunk):
                rows = slice(c * CHUNK, (c + 1) * CHUNK)
                last = c * CHUNK if p["rev"] else (c + 1) * CHUNK - 1
                gl = p["gc_c"][last:last + 1, :]
                p["wq"].append(jnp.concatenate([w[rows], qd[rows]], axis=0))
                p["kd"].append((kf[rows] * jnp.exp(gl - p["gc_c"][rows])).astype(BF16))
                p["egl"].append(jnp.exp(gl))
            p["state"] = s_ref[p["d"], p["h"]]
            p["o1"] = [None] * nchunk

        for t in range(nchunk):
            for i, p in enumerate(probs):
                c = nchunk - 1 - t if p["rev"] else t
                rows = slice(c * CHUNK, (c + 1) * CHUNK)
                pq = _dot(p["wq"][c], p["state"].astype(BF16))
                vnew = (p["u"][rows] - pq[0:CHUNK]).astype(BF16)
                p["o1"][c] = pq[CHUNK:2 * CHUNK]
                vn_ref[stage, i, rows, :] = vnew
                p["state"] = p["state"] * p["egl"][c] + lax.dot_general(
                    p["kd"][c], vnew, (((0,), (0,)), ((), ())), preferred_element_type=F32)

        for i, p in enumerate(probs):
            s_ref[p["d"], p["h"]] = p["state"]
            o = jnp.concatenate(p["o1"], axis=0) + _dot(p["qkg"], vn_ref[stage, i])
            rows = pl.ds(pl.multiple_of(p["row0"], bt), bt)
            o_ref[rows, p["h"] * LANES:(p["h"] + 1) * LANES] += o

    for stage in range(nstage):
        run_stage(stage)


def _deltanet(dnqkv, gcol, grow, B, S):
    T = dnqkv.shape[0]
    bt = DN_BT
    nb = S // bt
    fwd = lambda b, i: b * nb + i
    bwd = lambda b, i: b * nb + nb - 1 - i
    def specs(f):
        return [pl.BlockSpec((bt, 3 * DN_W), lambda b, i: (f(b, i), 0)),
                pl.BlockSpec((bt, 16), lambda b, i: (f(b, i), 0)),
                pl.BlockSpec((16, bt), lambda b, i: (0, f(b, i)))]
    return pl.pallas_call(
        _dn_kernel, grid=(B, nb),
        in_specs=specs(fwd) + specs(bwd),
        out_specs=pl.BlockSpec((S, DN_W), lambda b, i: (b, 0)),
        out_shape=jax.ShapeDtypeStruct((T, DN_W), F32),
        scratch_shapes=[pltpu.VMEM((2, N_DN, DK_DN, DK_DN), F32),
                        pltpu.VMEM((DN_BT // DN_SUB, 2 * N_DN, DN_SUB, DK_DN), BF16)],
        compiler_params=_cparams(("parallel", "arbitrary")), name="deltanet",
    )(dnqkv, gcol, grow, dnqkv, gcol, grow)


ATTN_TKB = 1024


def _attn_kernel(S, tq, tk, lam_init,
                 lam_ref, q_ref, k_ref, v_ref, g_ref, o_ref, s_scr, vaug):
    @pl.when(pl.program_id(2) == 0)
    def _():
        vaug[:, 0:LANES] = v_ref[...]
        vaug[:, LANES:2 * LANES] = jnp.ones((S, LANES), BF16)

    lf = lam_ref[...]
    lam = (jnp.exp(jnp.sum(lf[0:1] * lf[1:2], axis=-1, keepdims=True))
           - jnp.exp(jnp.sum(lf[2:3] * lf[3:4], axis=-1, keepdims=True)) + lam_init)
    q = q_ref[...]
    lane = lax.broadcasted_iota(jnp.int32, q.shape, 1)
    zero = jnp.zeros_like(q)
    qs = (jnp.where(lane < DH_DA, q, zero), jnp.where(lane < DH_DA, zero, q))

    m = []
    for c in range(2):
        mp = jnp.full((tq, LANES), -jnp.inf, F32)
        for i in range(S // tk):
            s = _dot_nt(qs[c], k_ref[i * tk:(i + 1) * tk, :])
            s_scr[c, :, i * tk:(i + 1) * tk] = s
            for j in range(tk // LANES):
                mp = jnp.maximum(mp, s[:, j * LANES:(j + 1) * LANES])
        m.append(jnp.max(mp, axis=-1, keepdims=True))

    outs = []
    for c in range(2):
        acc = jnp.zeros((tq, 2 * LANES), F32)
        for i in range(S // ATTN_TKB):
            sl = slice(i * ATTN_TKB, (i + 1) * ATTN_TKB)
            p = jnp.exp2(s_scr[c, :, sl] - m[c]).astype(BF16)
            acc = acc + _dot(p, vaug[sl, :])
        outs.append(acc)
    o1 = outs[0][:, 0:LANES] / outs[0][:, LANES:2 * LANES]
    o2 = outs[1][:, 0:LANES] / outs[1][:, LANES:2 * LANES]
    o_ref[...] = (_rms(o1 - lam * o2, g_ref[...]) * (1.0 - lam_init)).astype(BF16)


def _attn(aq, ak, av, da_lambda, subln_g, B, S, lam_init, tq=1024, tk=512):
    T = aq.shape[0]
    nq = S // tq
    kern = functools.partial(_attn_kernel, S, tq, tk, lam_init)
    kv_spec = pl.BlockSpec((S, LANES), lambda b, h, i: (b, h))
    q_spec = pl.BlockSpec((tq, LANES), lambda b, h, i: (b * nq + i, h))
    return pl.pallas_call(
        kern, grid=(B, N_DA, nq),
        in_specs=[_const_spec(da_lambda.shape), q_spec, kv_spec, kv_spec,
                  _const_spec(subln_g.shape)],
        out_specs=q_spec,
        out_shape=jax.ShapeDtypeStruct((T, DN_W), BF16),
        scratch_shapes=[pltpu.VMEM((2, tq, S), F32), pltpu.VMEM((S, 2 * LANES), BF16)],
        compiler_params=_cparams(("parallel", "parallel", "arbitrary")), name="attn",
    )(da_lambda, aq, ak, av, subln_g)


FFN_FC = 256
FFN_NC = D_FF // FFN_FC


def _ffn_kernel(tiles_per_seq, tm, *refs):
    (xp_ref, x_ref, xn_ref, op_ref, o_ref, on_ref, zp_ref, z_ref, zn_ref,
     ap_ref, a_ref, an_ref, pe_ref,
     dng_ref, wout_ref, ln2_ref, wg_ref, wu_ref, cw_ref, cb_ref, wd_ref,
     pproj_ref, png_ref, pgn_ref, wpg_ref, out_ref, gbuf, ubuf, act_scr) = refs
    j = pl.program_id(0) % tiles_per_seq
    cat = lambda p, m, n: jnp.concatenate([p[...], m[...], n[...]], axis=0)

    o = cat(op_ref, o_ref, on_ref)
    z = cat(zp_ref, z_ref, zn_ref)
    dng = dng_ref[...]
    parts = []
    for h in range(N_DN):
        cols = slice(h * LANES, (h + 1) * LANES)
        parts.append((_rms(o[:, cols], dng) * _silu(z[:, cols].astype(F32))).astype(BF16))
    parts.append(cat(ap_ref, a_ref, an_ref))
    x1 = cat(xp_ref, x_ref, xn_ref) + _dot(jnp.concatenate(parts, axis=1), wout_ref[...])
    rowi = lax.broadcasted_iota(jnp.int32, (tm + 2 * HALO, 1), 0)
    pad = ((rowi < HALO) & (j == 0)) | ((rowi >= HALO + tm) & (j == tiles_per_seq - 1))
    x1 = jnp.where(pad, 0.0, x1)

    hn = _rms(x1, ln2_ref[...]).astype(BF16)
    hm = hn[HALO:HALO + tm]
    r = FFN_CONV // 2

    def matmuls(c, slot):
        gbuf[slot] = _dot(hn, wg_ref[c])
        ubuf[slot] = _dot(hm, wu_ref[c])

    def activate(c, slot):
        cw = cw_ref[c]
        gate = gbuf[slot, pl.ds(HALO - r, tm), :] * cw[0:1]
        for t in range(1, FFN_CONV):
            gate = gate + gbuf[slot, pl.ds(HALO - r + t, tm), :] * cw[t:t + 1]
        gate = gate + cb_ref[c]
        act = (_silu(gate) * ubuf[slot]).astype(BF16)
        act_scr[:, c * FFN_FC:(c + 1) * FFN_FC] = act

    matmuls(0, 0)
    for c in range(1, FFN_NC):
        matmuls(c, c % 2)
        activate(c - 1, (c - 1) % 2)
    activate(FFN_NC - 1, (FFN_NC - 1) % 2)
    x2 = x1[HALO:HALO + tm] + _dot(act_scr[...], wd_ref[...])
    e = _rms(_dot(pe_ref[...].astype(BF16), pproj_ref[...]), png_ref[...])
    gt = jax.nn.sigmoid(_dot(_rms(x2, pgn_ref[...]).astype(BF16), wpg_ref[...]))
    out_ref[...] = x2 + gt * e


def _ffn(x, osum, z, oda, pe, S, prep, tm=512):
    T = x.shape[0]
    hb = tm // HALO
    nhb = T // HALO
    tps = S // tm
    kern = functools.partial(_ffn_kernel, tps, tm)
    tile = lambda w: pl.BlockSpec((tm, w), lambda i: (i, 0))

    def with_halo(w):
        return [pl.BlockSpec((HALO, w), lambda i: (jnp.maximum(i * hb - 1, 0), 0)), tile(w),
                pl.BlockSpec((HALO, w), lambda i: (jnp.minimum((i + 1) * hb, nhb - 1), 0))]

    names = ("dn_norm_g", "w_out", "ln2_g", "w_gate", "w_upp", "ffn_conv_w", "ffn_conv_b", "w_down",
             "ple_proj", "ple_norm_g", "ple_gate_norm_g", "w_ple_gate")
    in_specs = (with_halo(D_MODEL) + with_halo(DN_W) + with_halo(DN_W) + with_halo(DN_W)
                + [tile(PLE_DIM)]
                + [pl.BlockSpec(prep[n].shape, functools.partial(lambda nd, i: (0,) * nd, prep[n].ndim),
                                pipeline_mode=pl.Buffered(1)) for n in names])
    return pl.pallas_call(
        kern, grid=(T // tm,), in_specs=in_specs, out_specs=tile(D_MODEL),
        out_shape=jax.ShapeDtypeStruct((T, D_MODEL), F32),
        scratch_shapes=[pltpu.VMEM((2, tm + 2 * HALO, FFN_FC), F32),
                        pltpu.VMEM((2, tm, FFN_FC), F32),
                        pltpu.VMEM((tm, D_FF), BF16)],
        compiler_params=_cparams(("parallel",)), name="ffn",
    )(x, x, x, osum, osum, osum, z, z, z, oda, oda, oda, pe, *[prep[n] for n in names])


def _rope_tables(S):
    half = ROT_DIM // 2
    inv_freq = ROPE_THETA ** (-jnp.arange(0, ROT_DIM, 2, dtype=F32) / ROT_DIM)
    ang = jnp.arange(S, dtype=F32)[:, None] * inv_freq[None, :]
    cos, sin = jnp.cos(ang), jnp.sin(ang)
    zeros = jnp.zeros((S, DH_DA - ROT_DIM), F32)
    zh = jnp.zeros((S, half), F32)
    c = jnp.concatenate([cos, cos, zeros + 1.0], axis=1)
    s1 = jnp.concatenate([-sin, zh, zeros], axis=1)
    s2 = jnp.concatenate([zh, sin, zeros], axis=1)
    rep = lambda a: jnp.concatenate([a, a], axis=1)
    return rep(c), rep(s1), rep(s2)


def _prepare(layer, seqs, ln1_g, w_in, dn_conv_w, dn_a_log, dn_dt_bias, dn_norm_g, da_qk_norm_g,
             da_lambda, da_subln_g, w_out, ln2_g, w_up, ffn_conv_w, ffn_conv_b, w_down,
             ple_proj, ple_norm_g, ple_gate_norm_g, w_ple_gate):
    row = lambda a: a[layer].reshape(1, -1).astype(F32)
    w = w_in[layer]
    o_z, o_ab, o_aq = 3 * DN_W, 4 * DN_W, 4 * DN_W + 16
    w_rest = jnp.concatenate(
        [w[:, o_z:o_ab], w[:, o_aq:o_aq + 3 * DN_W], w[:, o_ab:o_aq],
         jnp.zeros((D_MODEL, LANES - 16), w.dtype)], axis=1)
    n8 = 2 * N_DN
    flat8 = lambda a: a[layer].reshape(n8).astype(F32)
    both = lambda a: jnp.zeros((n8, 2 * LANES), F32).at[0, 0:n8].set(flat8(a)).at[:, LANES].set(flat8(a))
    chunked = lambda a: a.reshape(a.shape[0], FFN_NC, FFN_FC).transpose(1, 0, 2)
    wu = w_up[layer]
    return {
        "ln1_g": row(ln1_g),
        "w_dn": w[:, 0:3 * DN_W].astype(BF16),
        "w_rest": w_rest.astype(BF16),
        "dn_conv_w": dn_conv_w[layer].astype(F32),
        "alog": both(dn_a_log), "dtb": both(dn_dt_bias),
        "qkg": jnp.tile(da_qk_norm_g[layer].astype(F32), (1, 2)),
        "rope": {S: _rope_tables(S) for S in seqs},
        "dn_norm_g": row(dn_norm_g),
        "da_lambda": da_lambda[layer].astype(F32),
        "da_subln_g": row(da_subln_g),
        "w_out": w_out[layer].astype(BF16),
        "ln2_g": row(ln2_g),
        "w_gate": chunked(wu[:, 0:D_FF]).astype(BF16),
        "w_upp": chunked(wu[:, D_FF:]).astype(BF16),
        "ffn_conv_w": chunked(ffn_conv_w[layer].astype(F32)),
        "ffn_conv_b": ffn_conv_b[layer].astype(F32).reshape(FFN_NC, 1, FFN_FC),
        "w_down": w_down[layer].astype(BF16),
        "ple_proj": ple_proj[layer].astype(BF16),
        "ple_norm_g": row(ple_norm_g),
        "ple_gate_norm_g": row(ple_gate_norm_g),
        "w_ple_gate": w_ple_gate[layer].astype(BF16),
    }


def _encoder_layer(x, pe, layer, prep):
    B, S, _ = x.shape
    lam_init = 0.8 - 0.6 * math.exp(-0.3 * layer)
    xf = x.reshape(B * S, D_MODEL)
    dnqkv, z, gcol, grow, aq, ak, av = _inproj(xf, S, prep)
    osum = _deltanet(dnqkv, gcol, grow, B, S)
    oda = _attn(aq, ak, av, prep["da_lambda"], prep["da_subln_g"], B, S, lam_init)
    out = _ffn(xf, osum, z, oda, pe.reshape(B * S, PLE_DIM), S, prep)
    return out.reshape(B, S, D_MODEL)


def kernel(x_prompt, x_sample, p_prompt, p_sample, ln1_g, w_in, dn_conv_w, dn_a_log, dn_dt_bias, dn_norm_g, da_qk_norm_g, da_lambda, da_subln_g, w_out, ln2_g, w_up, ffn_conv_w, ffn_conv_b, w_down, ple_proj, ple_norm_g, ple_gate_norm_g, w_ple_gate):
    weights = (ln1_g, w_in, dn_conv_w, dn_a_log, dn_dt_bias, dn_norm_g, da_qk_norm_g, da_lambda,
               da_subln_g, w_out, ln2_g, w_up, ffn_conv_w, ffn_conv_b, w_down, ple_proj,
               ple_norm_g, ple_gate_norm_g, w_ple_gate)
    depth = w_in.shape[0]
    seqs = (x_prompt.shape[1], x_sample.shape[1])
    outs = []
    for x, p in ((x_prompt, p_prompt), (x_sample, p_sample)):
        for layer in range(depth):
            prep = _prepare(layer, seqs, *weights)
            x = _encoder_layer(x, p[layer], layer, prep)
        outs.append(x)
    return tuple(outs)
```

```python
import functools
import math

import jax
import jax.numpy as jnp
from jax import lax
from jax.experimental import pallas as pl
from jax.experimental.pallas import tpu as pltpu

D_MODEL = 1024
N_DN = 4
DK_DN = 128
DN_CONV = 5
CHUNK = 64
N_DA = 4
DH_DA = 64
ROT_DIM = DH_DA // 4
ROPE_THETA = 500000.0
D_FF = 2816
FFN_CONV = 3
PLE_DIM = 256
EPS = 1e-6
DN_W = N_DN * DK_DN
LANES = 128
HALO = 16
VMEM_LIMIT = 56 * 1024 * 1024

F32 = jnp.float32
BF16 = jnp.bfloat16
LOG2E = math.log2(math.e)


def _dot(a, b):
    return jnp.dot(a, b, preferred_element_type=F32)


def _dot_nt(a, b):
    return lax.dot_general(a, b, (((1,), (1,)), ((), ())), preferred_element_type=F32)


def _rms(x, g):
    return x * lax.rsqrt(jnp.mean(x * x, axis=-1, keepdims=True) + EPS) * g


def _silu(x):
    return x * jax.nn.sigmoid(x)


def _cparams(sem):
    return pltpu.CompilerParams(dimension_semantics=sem, vmem_limit_bytes=VMEM_LIMIT)


def _const_spec(shape):
    nd = len(shape)
    return pl.BlockSpec(shape, lambda *_: (0,) * nd)


def _row_groups(tm):
    groups = ((0, 36), (288, 28))
    assert tm == sum(8 * s for _, s in groups)
    return groups


def _inproj_kernel(tiles_per_seq, tm,
                   xp_ref, x_ref, xn_ref, ln_ref, wdn_ref, wrest_ref, convw_ref,
                   alog_ref, dtb_ref, qkg_ref, rc_ref, rs1_ref, rs2_ref,
                   dn_ref, z_ref, gcol_ref, grow_ref, aq_ref, ak_ref, av_ref, pbuf, ybuf):
    j = pl.program_id(0) % tiles_per_seq
    ln = ln_ref[...]
    xp = jnp.where(j == 0, 0.0, xp_ref[...])
    xn = jnp.where(j == tiles_per_seq - 1, 0.0, xn_ref[...])
    xx = jnp.concatenate([xp, x_ref[...], xn], axis=0)
    hn = _rms(xx, ln).astype(BF16)
    hm = hn[HALO:HALO + tm]

    wrest = wrest_ref
    lane = lax.broadcasted_iota(jnp.int32, (tm, LANES), 1)
    lo = lane < DH_DA
    rc, rs1, rs2 = rc_ref[...], rs1_ref[...], rs2_ref[...]
    half = ROT_DIM // 2
    for which, out_ref in ((0, aq_ref), (1, ak_ref)):
        base = (1 + which) * DN_W
        g = qkg_ref[which:which + 1, :]
        for h2 in range(N_DA // 2):
            pair = _dot(hm, wrest[:, base + h2 * 2 * LANES:base + (h2 + 1) * 2 * LANES])
            for h in (2 * h2, 2 * h2 + 1):
                a = pair[:, (h % 2) * LANES:(h % 2 + 1) * LANES]
                sq = a * a
                s_lo = jnp.sum(jnp.where(lo, sq, 0.0), axis=-1, keepdims=True)
                s_hi = jnp.sum(jnp.where(lo, 0.0, sq), axis=-1, keepdims=True)
                inv = lax.rsqrt(jnp.where(lo, s_lo, s_hi) * (1.0 / DH_DA) + EPS)
                a = a * inv * g
                a = (a * rc + pltpu.roll(a, LANES - half, 1) * rs1
                     + pltpu.roll(a, half, 1) * rs2)
                if which == 0:
                    a = a * (DH_DA ** -0.5 * LOG2E)
                out_ref[:, h * LANES:(h + 1) * LANES] = a.astype(BF16)

    ab = _dot(hm, wrest[:, 4 * DN_W:4 * DN_W + LANES])
    abt = ab.T
    n8 = 2 * N_DN

    def gates(da, db, alog, dtb):
        g = -jnp.exp(alog) * jax.nn.softplus(da + dtb)
        return g, jax.nn.sigmoid(db)

    gc, bc = gates(ab[:, 0:n8], ab[:, n8:2 * n8], alog_ref[0:1, 0:n8], dtb_ref[0:1, 0:n8])
    gcol_ref[...] = jnp.concatenate([gc, bc], axis=1)
    gr, br = gates(abt[0:n8], abt[n8:2 * n8], alog_ref[:, LANES:LANES + 1][0:n8],
                   dtb_ref[:, LANES:LANES + 1][0:n8])
    grow_ref[...] = jnp.concatenate([gr, br], axis=0)

    for c2 in range(3 * N_DN // 2):
        res = _dot(hn, wdn_ref[:, c2 * 2 * LANES:(c2 + 1) * 2 * LANES])
        pbuf[2 * c2] = res[:, 0:LANES]
        pbuf[2 * c2 + 1] = res[:, LANES:2 * LANES]
    r = DN_CONV // 2
    for c in range(3 * N_DN):
        cols = slice(c * LANES, (c + 1) * LANES)
        taps = [convw_ref[t:t + 1, cols] for t in range(DN_CONV)]
        pieces = []
        for r0, s in _row_groups(tm):
            for b in range(s):
                acc = pbuf[c, pl.ds(HALO - r + r0 + b, 8, stride=s), :] * taps[0]
                for t in range(1, DN_CONV):
                    acc = acc + pbuf[c, pl.ds(HALO - r + t + r0 + b, 8, stride=s), :] * taps[t]
                pieces.append(acc)
        y = _silu(jnp.concatenate(pieces, axis=0))
        if c < 2 * N_DN:
            y = y * lax.rsqrt(jnp.sum(y * y, axis=-1, keepdims=True) + EPS)
            if c < N_DN:
                y = y * (DK_DN ** -0.5)
        n = 0
        for r0, s in _row_groups(tm):
            for b in range(s):
                ybuf[c, pl.ds(r0 + b, 8, stride=s), :] = y[8 * n:8 * n + 8]
                n += 1
        dn_ref[:, cols] = ybuf[c].astype(BF16)

    z_ref[...] = _dot(hm, wrest[:, 0:DN_W]).astype(BF16)
    av_ref[...] = _dot(hm, wrest[:, 3 * DN_W:4 * DN_W]).astype(BF16)


def _inproj(x, S, prep, tm=512):
    T = x.shape[0]
    assert S % tm == 0 and T % S == 0
    nt = T // tm
    hb = tm // HALO
    nhb = T // HALO
    tps = S // tm
    kern = functools.partial(_inproj_kernel, tps, tm)
    out_shape = (
        jax.ShapeDtypeStruct((T, 3 * DN_W), BF16),
        jax.ShapeDtypeStruct((T, DN_W), BF16),
        jax.ShapeDtypeStruct((T, 16), F32),
        jax.ShapeDtypeStruct((16, T), F32),
        jax.ShapeDtypeStruct((T, DN_W), BF16),
        jax.ShapeDtypeStruct((T, DN_W), BF16),
        jax.ShapeDtypeStruct((T, DN_W), BF16),
    )
    tile = lambda w: pl.BlockSpec((tm, w), lambda i: (i, 0))
    in_specs = [
        pl.BlockSpec((HALO, D_MODEL), lambda i: (jnp.maximum(i * hb - 1, 0), 0)),
        tile(D_MODEL),
        pl.BlockSpec((HALO, D_MODEL), lambda i: (jnp.minimum((i + 1) * hb, nhb - 1), 0)),
        _const_spec((1, D_MODEL)),
        _const_spec(prep["w_dn"].shape),
        _const_spec(prep["w_rest"].shape),
        _const_spec(prep["dn_conv_w"].shape),
        _const_spec(prep["alog"].shape),
        _const_spec(prep["dtb"].shape),
        _const_spec(prep["qkg"].shape),
        pl.BlockSpec((tm, LANES), lambda i: (i % tps, 0)),
        pl.BlockSpec((tm, LANES), lambda i: (i % tps, 0)),
        pl.BlockSpec((tm, LANES), lambda i: (i % tps, 0)),
    ]
    out_specs = (tile(3 * DN_W), tile(DN_W), tile(16),
                 pl.BlockSpec((16, tm), lambda i: (0, i)),
                 tile(DN_W), tile(DN_W), tile(DN_W))
    rc, rs1, rs2 = prep["rope"][S]
    return pl.pallas_call(
        kern, grid=(nt,), in_specs=in_specs, out_specs=out_specs, out_shape=out_shape,
        scratch_shapes=[pltpu.VMEM((3 * N_DN, tm + 2 * HALO, LANES), F32),
                        pltpu.VMEM((3 * N_DN, tm, LANES), F32)],
        compiler_params=_cparams(("parallel",)), name="inproj",
    )(x, x, x, prep["ln1_g"], prep["w_dn"], prep["w_rest"], prep["dn_conv_w"],
      prep["alog"], prep["dtb"], prep["qkg"], rc, rs1, rs2)


DN_TILE = 128
DN_SUB = 256
DN_BT = 512


def _chunk_cumsum(x, axis, reverse):
    n = x.shape[axis]
    pos = lax.broadcasted_iota(jnp.int32, x.shape, axis) % CHUNK
    s = 1
    while s < CHUNK:
        if reverse:
            sh = pltpu.roll(x, n - s, axis)
            x = x + jnp.where(pos < CHUNK - s, sh, 0.0)
        else:
            sh = pltpu.roll(x, s, axis)
            x = x + jnp.where(pos >= s, sh, 0.0)
        s *= 2
    return x


def _dn_kernel(qkvf_ref, gcf_ref, grf_ref, qkvb_ref, gcb_ref, grb_ref,
               o_ref, s_ref, vn_ref):
    bt = DN_TILE
    nchunk = bt // CHUNK
    ntile = DN_SUB // DN_TILE
    nstage = DN_BT // DN_SUB
    step = pl.program_id(1)
    nb = pl.num_programs(1)

    @pl.when(step == 0)
    def _():
        s_ref[...] = jnp.zeros_like(s_ref)
        o_ref[...] = jnp.zeros_like(o_ref)

    row = lax.broadcasted_iota(jnp.int32, (bt, bt), 0)
    col = lax.broadcasted_iota(jnp.int32, (bt, bt), 1)
    same = (row // CHUNK) == (col // CHUNK)
    eye = (row == col).astype(F32)

    def level(b):
        return ((row // (2 * b)) == (col // (2 * b))) & ((row // b) != (col // b))

    def run_stage(stage):
        probs = {}
        dirs = ((qkvf_ref, gcf_ref, grf_ref, step), (qkvb_ref, gcb_ref, grb_ref, nb - 1 - step))
        for d, (qkv_ref, gc_ref, gr_ref, blk) in enumerate(dirs):
            rev = d == 1
            sub = nstage - 1 - stage if rev else stage
            gcol = gc_ref[sub * DN_SUB:(sub + 1) * DN_SUB, :]
            grow = gr_ref[:, sub * DN_SUB:(sub + 1) * DN_SUB]
            cum_c = _chunk_cumsum(gcol, 0, rev)
            cum_r = _chunk_cumsum(grow, 1, rev)
            if rev:
                incl = same & (row <= col)
                strict = same & (row < col)
            else:
                incl = same & (row >= col)
                strict = same & (row > col)
            for h in range(N_DN):
                ci = d * N_DN + h
                for u in range(ntile):
                    loc = slice(u * bt, (u + 1) * bt)
                    tok = slice(sub * DN_SUB + u * bt, sub * DN_SUB + (u + 1) * bt)
                    probs[d, h, u] = dict(
                        rev=rev, row0=blk * DN_BT + sub * DN_SUB + u * bt, incl=incl, strict=strict,
                        gc_c=cum_c[loc, ci:ci + 1], gc_r=cum_r[ci:ci + 1, loc],
                        b_c=gcol[loc, 2 * N_DN + ci:2 * N_DN + ci + 1],
                        q=qkv_ref[tok, h * LANES:(h + 1) * LANES],
                        k=qkv_ref[tok, DN_W + h * LANES:DN_W + (h + 1) * LANES],
                        v=qkv_ref[tok, 2 * DN_W + h * LANES:2 * DN_W + (h + 1) * LANES])
        plist = list(probs.values())

        for p in plist:
            gam = jnp.exp(jnp.where(p["incl"], p["gc_c"] - p["gc_r"], -jnp.inf))
            kk = _dot_nt(p["k"], p["k"])
            qk = _dot_nt(p["q"], p["k"])
            a = jnp.where(p["strict"], p["b_c"] * kk * gam, 0.0)
            p["ab"] = a.astype(BF16)
            p["qkg"] = (qk * gam).astype(BF16)
            p["db"] = (eye - jnp.where(level(1), a, 0.0)).astype(BF16)

        b = 2
        while b < CHUNK:
            lm = level(b)
            for p in plist:
                p["x"] = jnp.where(lm, _dot(p["db"], p["ab"]), 0.0).astype(BF16)
            for p in plist:
                p["db"] = jnp.where(lm, (-_dot(p["x"], p["db"])).astype(BF16), p["db"])
            b *= 2

        for p in plist:
            e_c = jnp.exp(p["gc_c"])
            kf = p["k"].astype(F32)
            vb = (p["v"].astype(F32) * p["b_c"]).astype(BF16)
            ke = (kf * (e_c * p["b_c"])).astype(BF16)
            qd = (p["q"].astype(F32) * e_c).astype(BF16)
            uw = _dot(p["db"], jnp.concatenate([vb, ke], axis=1))
            p["u"] = uw[:, 0:LANES]
            w = uw[:, LANES:2 * LANES].astype(BF16)
            p["wq"], p["kd"], p["egl"] = [], [], []
            for c in range(nchunk):
                rows = slice(c * CHUNK, (c + 1) * CHUNK)
                last = c * CHUNK if p["rev"] else (c + 1) * CHUNK - 1
                gl = p["gc_c"][last:last + 1, :]
                p["wq"].append(jnp.concatenate([w[rows], qd[rows]], axis=0))
                p["kd"].append((kf[rows] * jnp.exp(gl - p["gc_c"][rows])).astype(BF16))
                p["egl"].append(jnp.exp(gl))
            p["o1"] = [None] * nchunk

        states = {(d, h): s_ref[d, h] for d in range(2) for h in range(N_DN)}
        for t in range(ntile * nchunk):
            for (d, h), state in states.items():
                u, c = t // nchunk, t % nchunk
                if d == 1:
                    u, c = ntile - 1 - u, nchunk - 1 - c
                p = probs[d, h, u]
                i = (d * N_DN + h) * ntile + u
                rows = slice(c * CHUNK, (c + 1) * CHUNK)
                pq = _dot(p["wq"][c], state.astype(BF16))
                vnew = (p["u"][rows] - pq[0:CHUNK]).astype(BF16)
                p["o1"][c] = pq[CHUNK:2 * CHUNK]
                vn_ref[stage, i, rows, :] = vnew
                states[d, h] = state * p["egl"][c] + lax.dot_general(
                    p["kd"][c], vnew, (((0,), (0,)), ((), ())), preferred_element_type=F32)

        for (d, h), state in states.items():
            s_ref[d, h] = state
        for (d, h, u), p in probs.items():
            i = (d * N_DN + h) * ntile + u
            o = jnp.concatenate(p["o1"], axis=0) + _dot(p["qkg"], vn_ref[stage, i])
            rows = pl.ds(pl.multiple_of(p["row0"], bt), bt)
            o_ref[rows, h * LANES:(h + 1) * LANES] += o

    for stage in range(nstage):
        run_stage(stage)


def _deltanet(dnqkv, gcol, grow, B, S):
    T = dnqkv.shape[0]
    bt = DN_BT
    nb = S // bt
    fwd = lambda b, i: b * nb + i
    bwd = lambda b, i: b * nb + nb - 1 - i
    def specs(f):
        return [pl.BlockSpec((bt, 3 * DN_W), lambda b, i: (f(b, i), 0)),
                pl.BlockSpec((bt, 16), lambda b, i: (f(b, i), 0)),
                pl.BlockSpec((16, bt), lambda b, i: (0, f(b, i)))]
    return pl.pallas_call(
        _dn_kernel, grid=(B, nb),
        in_specs=specs(fwd) + specs(bwd),
        out_specs=pl.BlockSpec((S, DN_W), lambda b, i: (b, 0)),
        out_shape=jax.ShapeDtypeStruct((T, DN_W), F32),
        scratch_shapes=[pltpu.VMEM((2, N_DN, DK_DN, DK_DN), F32),
                        pltpu.VMEM((DN_BT // DN_SUB, 2 * N_DN * (DN_SUB // DN_TILE), DN_TILE, DK_DN), BF16)],
        compiler_params=_cparams(("parallel", "arbitrary")), name="deltanet",
    )(dnqkv, gcol, grow, dnqkv, gcol, grow)


ATTN_TKB = 1024


def _attn_kernel(S, tq, tk, lam_init,
                 lam_ref, q_ref, k_ref, v_ref, g_ref, o_ref, s_scr, vaug):
    @pl.when(pl.program_id(2) == 0)
    def _():
        vaug[:, 0:LANES] = v_ref[...]
        vaug[:, LANES:2 * LANES] = jnp.ones((S, LANES), BF16)

    lf = lam_ref[...]
    lam = (jnp.exp(jnp.sum(lf[0:1] * lf[1:2], axis=-1, keepdims=True))
           - jnp.exp(jnp.sum(lf[2:3] * lf[3:4], axis=-1, keepdims=True)) + lam_init)
    q = q_ref[...]
    lane = lax.broadcasted_iota(jnp.int32, q.shape, 1)
    zero = jnp.zeros_like(q)
    qs = (jnp.where(lane < DH_DA, q, zero), jnp.where(lane < DH_DA, zero, q))

    m = []
    for c in range(2):
        mp = jnp.full((tq, LANES), -jnp.inf, F32)
        for i in range(S // tk):
            s = _dot_nt(qs[c], k_ref[i * tk:(i + 1) * tk, :])
            s_scr[c, :, i * tk:(i + 1) * tk] = s
            for j in range(tk // LANES):
                mp = jnp.maximum(mp, s[:, j * LANES:(j + 1) * LANES])
        m.append(jnp.max(mp, axis=-1, keepdims=True))

    outs = []
    for c in range(2):
        acc = jnp.zeros((tq, 2 * LANES), F32)
        for i in range(S // ATTN_TKB):
            sl = slice(i * ATTN_TKB, (i + 1) * ATTN_TKB)
            p = jnp.exp2(s_scr[c, :, sl] - m[c]).astype(BF16)
            acc = acc + _dot(p, vaug[sl, :])
        outs.append(acc)
    o1 = outs[0][:, 0:LANES] / outs[0][:, LANES:2 * LANES]
    o2 = outs[1][:, 0:LANES] / outs[1][:, LANES:2 * LANES]
    o_ref[...] = (_rms(o1 - lam * o2, g_ref[...]) * (1.0 - lam_init)).astype(BF16)


def _attn(aq, ak, av, da_lambda, subln_g, B, S, lam_init, tq=1024, tk=512):
    T = aq.shape[0]
    nq = S // tq
    kern = functools.partial(_attn_kernel, S, tq, tk, lam_init)
    kv_spec = pl.BlockSpec((S, LANES), lambda b, h, i: (b, h))
    q_spec = pl.BlockSpec((tq, LANES), lambda b, h, i: (b * nq + i, h))
    return pl.pallas_call(
        kern, grid=(B, N_DA, nq),
        in_specs=[_const_spec(da_lambda.shape), q_spec, kv_spec, kv_spec,
                  _const_spec(subln_g.shape)],
        out_specs=q_spec,
        out_shape=jax.ShapeDtypeStruct((T, DN_W), BF16),
        scratch_shapes=[pltpu.VMEM((2, tq, S), F32), pltpu.VMEM((S, 2 * LANES), BF16)],
        compiler_params=_cparams(("parallel", "parallel", "arbitrary")), name="attn",
    )(da_lambda, aq, ak, av, subln_g)


FFN_FC = 256
FFN_NC = D_FF // FFN_FC


def _ffn_kernel(tiles_per_seq, tm, *refs):
    (xp_ref, x_ref, xn_ref, op_ref, o_ref, on_ref, zp_ref, z_ref, zn_ref,
     ap_ref, a_ref, an_ref, pe_ref,
     dng_ref, wout_ref, ln2_ref, wg_ref, wu_ref, cw_ref, cb_ref, wd_ref,
     pproj_ref, png_ref, pgn_ref, wpg_ref, out_ref, gbuf, ubuf, act_scr) = refs
    j = pl.program_id(0) % tiles_per_seq
    cat = lambda p, m, n: jnp.concatenate([p[...], m[...], n[...]], axis=0)

    o = cat(op_ref, o_ref, on_ref)
    z = cat(zp_ref, z_ref, zn_ref)
    dng = dng_ref[...]
    parts = []
    for h in range(N_DN):
        cols = slice(h * LANES, (h + 1) * LANES)
        parts.append((_rms(o[:, cols], dng) * _silu(z[:, cols].astype(F32))).astype(BF16))
    parts.append(cat(ap_ref, a_ref, an_ref))
    x1 = cat(xp_ref, x_ref, xn_ref) + _dot(jnp.concatenate(parts, axis=1), wout_ref[...])
    rowi = lax.broadcasted_iota(jnp.int32, (tm + 2 * HALO, 1), 0)
    pad = ((rowi < HALO) & (j == 0)) | ((rowi >= HALO + tm) & (j == tiles_per_seq - 1))
    x1 = jnp.where(pad, 0.0, x1)

    hn = _rms(x1, ln2_ref[...]).astype(BF16)
    hm = hn[HALO:HALO + tm]
    r = FFN_CONV // 2

    def matmuls(c, slot):
        gbuf[slot] = _dot(hn, wg_ref[c])
        ubuf[slot] = _dot(hm, wu_ref[c])

    def activate(c, slot):
        cw = cw_ref[c]
        gate = gbuf[slot, pl.ds(HALO - r, tm), :] * cw[0:1]
        for t in range(1, FFN_CONV):
            gate = gate + gbuf[slot, pl.ds(HALO - r + t, tm), :] * cw[t:t + 1]
        gate = gate + cb_ref[c]
        act = (_silu(gate) * ubuf[slot]).astype(BF16)
        act_scr[:, c * FFN_FC:(c + 1) * FFN_FC] = act

    matmuls(0, 0)
    for c in range(1, FFN_NC):
        matmuls(c, c % 2)
        activate(c - 1, (c - 1) % 2)
    activate(FFN_NC - 1, (FFN_NC - 1) % 2)
    x2 = x1[HALO:HALO + tm] + _dot(act_scr[...], wd_ref[...])
    e = _rms(_dot(pe_ref[...].astype(BF16), pproj_ref[...]), png_ref[...])
    gt = jax.nn.sigmoid(_dot(_rms(x2, pgn_ref[...]).astype(BF16), wpg_ref[...]))
    out_ref[...] = x2 + gt * e


def _ffn(x, osum, z, oda, pe, pe_row0, S, prep, tm=512):
    T = x.shape[0]
    hb = tm // HALO
    nhb = T // HALO
    tps = S // tm
    kern = functools.partial(_ffn_kernel, tps, tm)
    tile = lambda w: pl.BlockSpec((tm, w), lambda i: (i, 0))

    def with_halo(w):
        return [pl.BlockSpec((HALO, w), lambda i: (jnp.maximum(i * hb - 1, 0), 0)), tile(w),
                pl.BlockSpec((HALO, w), lambda i: (jnp.minimum((i + 1) * hb, nhb - 1), 0))]

    names = ("dn_norm_g", "w_out", "ln2_g", "w_gate", "w_upp", "ffn_conv_w", "ffn_conv_b", "w_down",
             "ple_proj", "ple_norm_g", "ple_gate_norm_g", "w_ple_gate")
    in_specs = (with_halo(D_MODEL) + with_halo(DN_W) + with_halo(DN_W) + with_halo(DN_W)
                + [pl.BlockSpec((tm, PLE_DIM), lambda i: (i + pe_row0 // tm, 0))]
                + [pl.BlockSpec(prep[n].shape, functools.partial(lambda nd, i: (0,) * nd, prep[n].ndim),
                                pipeline_mode=pl.Buffered(1)) for n in names])
    return pl.pallas_call(
        kern, grid=(T // tm,), in_specs=in_specs, out_specs=tile(D_MODEL),
        out_shape=jax.ShapeDtypeStruct((T, D_MODEL), F32),
        scratch_shapes=[pltpu.VMEM((2, tm + 2 * HALO, FFN_FC), F32),
                        pltpu.VMEM((2, tm, FFN_FC), F32),
                        pltpu.VMEM((tm, D_FF), BF16)],
        compiler_params=_cparams(("parallel",)), name="ffn",
    )(x, x, x, osum, osum, osum, z, z, z, oda, oda, oda, pe, *[prep[n] for n in names])


def _rope_tables(S):
    half = ROT_DIM // 2
    inv_freq = ROPE_THETA ** (-jnp.arange(0, ROT_DIM, 2, dtype=F32) / ROT_DIM)
    ang = jnp.arange(S, dtype=F32)[:, None] * inv_freq[None, :]
    cos, sin = jnp.cos(ang), jnp.sin(ang)
    zeros = jnp.zeros((S, DH_DA - ROT_DIM), F32)
    zh = jnp.zeros((S, half), F32)
    c = jnp.concatenate([cos, cos, zeros + 1.0], axis=1)
    s1 = jnp.concatenate([-sin, zh, zeros], axis=1)
    s2 = jnp.concatenate([zh, sin, zeros], axis=1)
    rep = lambda a: jnp.concatenate([a, a], axis=1)
    return rep(c), rep(s1), rep(s2)


def _prepare(layer, seqs, ln1_g, w_in, dn_conv_w, dn_a_log, dn_dt_bias, dn_norm_g, da_qk_norm_g,
             da_lambda, da_subln_g, w_out, ln2_g, w_up, ffn_conv_w, ffn_conv_b, w_down,
             ple_proj, ple_norm_g, ple_gate_norm_g, w_ple_gate):
    row = lambda a: a[layer].reshape(1, -1).astype(F32)
    w = w_in[layer]
    o_z, o_ab, o_aq = 3 * DN_W, 4 * DN_W, 4 * DN_W + 16
    w_rest = jnp.concatenate(
        [w[:, o_z:o_ab], w[:, o_aq:o_aq + 3 * DN_W], w[:, o_ab:o_aq],
         jnp.zeros((D_MODEL, LANES - 16), w.dtype)], axis=1)
    n8 = 2 * N_DN
    flat8 = lambda a: a[layer].reshape(n8).astype(F32)
    both = lambda a: jnp.zeros((n8, 2 * LANES), F32).at[0, 0:n8].set(flat8(a)).at[:, LANES].set(flat8(a))
    chunked = lambda a: a.reshape(a.shape[0], FFN_NC, FFN_FC).transpose(1, 0, 2)
    wu = w_up[layer]
    return {
        "ln1_g": row(ln1_g),
        "w_dn": w[:, 0:3 * DN_W].astype(BF16),
        "w_rest": w_rest.astype(BF16),
        "dn_conv_w": dn_conv_w[layer].astype(F32),
        "alog": both(dn_a_log), "dtb": both(dn_dt_bias),
        "qkg": jnp.tile(da_qk_norm_g[layer].astype(F32), (1, 2)),
        "rope": {S: _rope_tables(S) for S in seqs},
        "dn_norm_g": row(dn_norm_g),
        "da_lambda": da_lambda[layer].astype(F32),
        "da_subln_g": row(da_subln_g),
        "w_out": w_out[layer].astype(BF16),
        "ln2_g": row(ln2_g),
        "w_gate": chunked(wu[:, 0:D_FF]).astype(BF16),
        "w_upp": chunked(wu[:, D_FF:]).astype(BF16),
        "ffn_conv_w": chunked(ffn_conv_w[layer].astype(F32)),
        "ffn_conv_b": ffn_conv_b[layer].astype(F32).reshape(FFN_NC, 1, FFN_FC),
        "w_down": w_down[layer].astype(BF16),
        "ple_proj": ple_proj[layer].astype(BF16),
        "ple_norm_g": row(ple_norm_g),
        "ple_gate_norm_g": row(ple_gate_norm_g),
        "w_ple_gate": w_ple_gate[layer].astype(BF16),
    }


def _encoder_layer(x, pe_all, layer, prep):
    B, S, _ = x.shape
    lam_init = 0.8 - 0.6 * math.exp(-0.3 * layer)
    xf = x.reshape(B * S, D_MODEL)
    dnqkv, z, gcol, grow, aq, ak, av = _inproj(xf, S, prep)
    osum = _deltanet(dnqkv, gcol, grow, B, S)
    oda = _attn(aq, ak, av, prep["da_lambda"], prep["da_subln_g"], B, S, lam_init)
    out = _ffn(xf, osum, z, oda, pe_all.reshape(-1, PLE_DIM), layer * B * S, S, prep)
    return out.reshape(B, S, D_MODEL)


def kernel(x_prompt, x_sample, p_prompt, p_sample, ln1_g, w_in, dn_conv_w, dn_a_log, dn_dt_bias, dn_norm_g, da_qk_norm_g, da_lambda, da_subln_g, w_out, ln2_g, w_up, ffn_conv_w, ffn_conv_b, w_down, ple_proj, ple_norm_g, ple_gate_norm_g, w_ple_gate):
    weights = (ln1_g, w_in, dn_conv_w, dn_a_log, dn_dt_bias, dn_norm_g, da_qk_norm_g, da_lambda,
               da_subln_g, w_out, ln2_g, w_up, ffn_conv_w, ffn_conv_b, w_down, ple_proj,
               ple_norm_g, ple_gate_norm_g, w_ple_gate)
    depth = w_in.shape[0]
    seqs = (x_prompt.shape[1], x_sample.shape[1])
    outs = []
    for x, p in ((x_prompt, p_prompt), (x_sample, p_sample)):
        for layer in range(depth):
            prep = _prepare(layer, seqs, *weights)
            x = _encoder_layer(x, p, layer, prep)
        outs.append(x)
    return tuple(outs)
```

```python
import functools
import math

import jax
import jax.numpy as jnp
from jax import lax
from jax.experimental import pallas as pl
from jax.experimental.pallas import tpu as pltpu

D_MODEL = 1024
N_DN = 4
DK_DN = 128
DN_CONV = 5
CHUNK = 64
N_DA = 4
DH_DA = 64
ROT_DIM = DH_DA // 4
ROPE_THETA = 500000.0
D_FF = 2816
FFN_CONV = 3
PLE_DIM = 256
EPS = 1e-6
DN_W = N_DN * DK_DN
LANES = 128
HALO = 16
VMEM_LIMIT = 56 * 1024 * 1024

F32 = jnp.float32
BF16 = jnp.bfloat16
LOG2E = math.log2(math.e)


def _dot(a, b):
    return jnp.dot(a, b, preferred_element_type=F32)


def _dot_nt(a, b):
    return lax.dot_general(a, b, (((1,), (1,)), ((), ())), preferred_element_type=F32)


def _rms(x, g):
    return x * lax.rsqrt(jnp.mean(x * x, axis=-1, keepdims=True) + EPS) * g


def _silu(x):
    return x * jax.nn.sigmoid(x)


def _cparams(sem):
    return pltpu.CompilerParams(dimension_semantics=sem, vmem_limit_bytes=VMEM_LIMIT)


def _const_spec(shape):
    nd = len(shape)
    return pl.BlockSpec(shape, lambda *_: (0,) * nd)


def _row_groups(tm):
    groups = ((0, 36), (288, 28))
    assert tm == sum(8 * s for _, s in groups)
    return groups


def _inproj_kernel(tiles_per_seq, tm,
                   xp_ref, x_ref, xn_ref, ln_ref, wdn_ref, wrest_ref, convw_ref,
                   alog_ref, dtb_ref, qkg_ref, rc_ref, rs1_ref, rs2_ref,
                   dn_ref, z_ref, gcol_ref, grow_ref, aq_ref, ak_ref, av_ref, pbuf, ybuf):
    j = pl.program_id(0) % tiles_per_seq
    ln = ln_ref[...]
    xp = jnp.where(j == 0, 0.0, xp_ref[...])
    xn = jnp.where(j == tiles_per_seq - 1, 0.0, xn_ref[...])
    xx = jnp.concatenate([xp, x_ref[...], xn], axis=0)
    hn = _rms(xx, ln).astype(BF16)
    hm = hn[HALO:HALO + tm]

    wrest = wrest_ref
    lane = lax.broadcasted_iota(jnp.int32, (tm, LANES), 1)
    lo = lane < DH_DA
    rc, rs1, rs2 = rc_ref[...], rs1_ref[...], rs2_ref[...]
    half = ROT_DIM // 2
    for which, out_ref in ((0, aq_ref), (1, ak_ref)):
        base = (1 + which) * DN_W
        g = qkg_ref[which:which + 1, :]
        for h2 in range(N_DA // 2):
            pair = _dot(hm, wrest[:, base + h2 * 2 * LANES:base + (h2 + 1) * 2 * LANES])
            for h in (2 * h2, 2 * h2 + 1):
                a = pair[:, (h % 2) * LANES:(h % 2 + 1) * LANES]
                sq = a * a
                s_lo = jnp.sum(jnp.where(lo, sq, 0.0), axis=-1, keepdims=True)
                s_hi = jnp.sum(jnp.where(lo, 0.0, sq), axis=-1, keepdims=True)
                inv = lax.rsqrt(jnp.where(lo, s_lo, s_hi) * (1.0 / DH_DA) + EPS)
                a = a * inv * g
                a = (a * rc + pltpu.roll(a, LANES - half, 1) * rs1
                     + pltpu.roll(a, half, 1) * rs2)
                if which == 0:
                    a = a * (DH_DA ** -0.5 * LOG2E)
                out_ref[:, h * LANES:(h + 1) * LANES] = a.astype(BF16)

    ab = _dot(hm, wrest[:, 4 * DN_W:4 * DN_W + LANES])
    abt = ab.T
    n8 = 2 * N_DN

    def gates(da, db, alog, dtb):
        g = -jnp.exp(alog) * jax.nn.softplus(da + dtb)
        return g, jax.nn.sigmoid(db)

    gc, bc = gates(ab[:, 0:n8], ab[:, n8:2 * n8], alog_ref[0:1, 0:n8], dtb_ref[0:1, 0:n8])
    gcol_ref[...] = jnp.concatenate([gc, bc], axis=1)
    gr, br = gates(abt[0:n8], abt[n8:2 * n8], alog_ref[:, LANES:LANES + 1][0:n8],
                   dtb_ref[:, LANES:LANES + 1][0:n8])
    grow_ref[...] = jnp.concatenate([gr, br], axis=0)

    for c2 in range(3 * N_DN // 2):
        res = _dot(hn, wdn_ref[:, c2 * 2 * LANES:(c2 + 1) * 2 * LANES])
        pbuf[2 * c2] = res[:, 0:LANES]
        pbuf[2 * c2 + 1] = res[:, LANES:2 * LANES]
    r = DN_CONV // 2
    for c in range(3 * N_DN):
        cols = slice(c * LANES, (c + 1) * LANES)
        taps = [convw_ref[t:t + 1, cols] for t in range(DN_CONV)]
        pieces = []
        for r0, s in _row_groups(tm):
            for b in range(s):
                acc = pbuf[c, pl.ds(HALO - r + r0 + b, 8, stride=s), :] * taps[0]
                for t in range(1, DN_CONV):
                    acc = acc + pbuf[c, pl.ds(HALO - r + t + r0 + b, 8, stride=s), :] * taps[t]
                pieces.append(acc)
        y = _silu(jnp.concatenate(pieces, axis=0))
        if c < 2 * N_DN:
            y = y * lax.rsqrt(jnp.sum(y * y, axis=-1, keepdims=True) + EPS)
            if c < N_DN:
                y = y * (DK_DN ** -0.5)
        n = 0
        for r0, s in _row_groups(tm):
            for b in range(s):
                ybuf[c, pl.ds(r0 + b, 8, stride=s), :] = y[8 * n:8 * n + 8]
                n += 1
        dn_ref[:, cols] = ybuf[c].astype(BF16)

    z_ref[...] = _dot(hm, wrest[:, 0:DN_W]).astype(BF16)
    av_ref[...] = _dot(hm, wrest[:, 3 * DN_W:4 * DN_W]).astype(BF16)


def _inproj(x, S, prep, tm=512):
    T = x.shape[0]
    assert S % tm == 0 and T % S == 0
    nt = T // tm
    hb = tm // HALO
    nhb = T // HALO
    tps = S // tm
    kern = functools.partial(_inproj_kernel, tps, tm)
    out_shape = (
        jax.ShapeDtypeStruct((T, 3 * DN_W), BF16),
        jax.ShapeDtypeStruct((T, DN_W), BF16),
        jax.ShapeDtypeStruct((T, 16), F32),
        jax.ShapeDtypeStruct((16, T), F32),
        jax.ShapeDtypeStruct((T, DN_W), BF16),
        jax.ShapeDtypeStruct((T, DN_W), BF16),
        jax.ShapeDtypeStruct((T, DN_W), BF16),
    )
    tile = lambda w: pl.BlockSpec((tm, w), lambda i: (i, 0))
    in_specs = [
        pl.BlockSpec((HALO, D_MODEL), lambda i: (jnp.maximum(i * hb - 1, 0), 0)),
        tile(D_MODEL),
        pl.BlockSpec((HALO, D_MODEL), lambda i: (jnp.minimum((i + 1) * hb, nhb - 1), 0)),
        _const_spec((1, D_MODEL)),
        _const_spec(prep["w_dn"].shape),
        _const_spec(prep["w_rest"].shape),
        _const_spec(prep["dn_conv_w"].shape),
        _const_spec(prep["alog"].shape),
        _const_spec(prep["dtb"].shape),
        _const_spec(prep["qkg"].shape),
        pl.BlockSpec((tm, LANES), lambda i: (i % tps, 0)),
        pl.BlockSpec((tm, LANES), lambda i: (i % tps, 0)),
        pl.BlockSpec((tm, LANES), lambda i: (i % tps, 0)),
    ]
    out_specs = (tile(3 * DN_W), tile(DN_W), tile(16),
                 pl.BlockSpec((16, tm), lambda i: (0, i)),
                 tile(DN_W), tile(DN_W), tile(DN_W))
    rc, rs1, rs2 = prep["rope"][S]
    return pl.pallas_call(
        kern, grid=(nt,), in_specs=in_specs, out_specs=out_specs, out_shape=out_shape,
        scratch_shapes=[pltpu.VMEM((3 * N_DN, tm + 2 * HALO, LANES), F32),
                        pltpu.VMEM((3 * N_DN, tm, LANES), F32)],
        compiler_params=_cparams(("parallel",)), name="inproj",
    )(x, x, x, prep["ln1_g"], prep["w_dn"], prep["w_rest"], prep["dn_conv_w"],
      prep["alog"], prep["dtb"], prep["qkg"], rc, rs1, rs2)


DN_TILE = 128
DN_SUB = 256
DN_BT = 1024


def _chunk_cumsum(x, axis, reverse):
    n = x.shape[axis]
    pos = lax.broadcasted_iota(jnp.int32, x.shape, axis) % CHUNK
    s = 1
    while s < CHUNK:
        if reverse:
            sh = pltpu.roll(x, n - s, axis)
            x = x + jnp.where(pos < CHUNK - s, sh, 0.0)
        else:
            sh = pltpu.roll(x, s, axis)
            x = x + jnp.where(pos >= s, sh, 0.0)
        s *= 2
    return x


def _dn_kernel(qkvf_ref, gcf_ref, grf_ref, qkvb_ref, gcb_ref, grb_ref,
               o_ref, s_ref, vn_ref):
    bt = DN_TILE
    nchunk = bt // CHUNK
    ntile = DN_SUB // DN_TILE
    nstage = DN_BT // DN_SUB
    step = pl.program_id(1)
    nb = pl.num_programs(1)

    @pl.when(step == 0)
    def _():
        s_ref[...] = jnp.zeros_like(s_ref)
        o_ref[...] = jnp.zeros_like(o_ref)

    row = lax.broadcasted_iota(jnp.int32, (bt, bt), 0)
    col = lax.broadcasted_iota(jnp.int32, (bt, bt), 1)
    same = (row // CHUNK) == (col // CHUNK)
    eye = (row == col).astype(F32)

    def level(b):
        return ((row // (2 * b)) == (col // (2 * b))) & ((row // b) != (col // b))

    def run_stage(stage):
        probs = {}
        dirs = ((qkvf_ref, gcf_ref, grf_ref, step), (qkvb_ref, gcb_ref, grb_ref, nb - 1 - step))
        for d, (qkv_ref, gc_ref, gr_ref, blk) in enumerate(dirs):
            rev = d == 1
            sub = nstage - 1 - stage if rev else stage
            gcol = gc_ref[sub * DN_SUB:(sub + 1) * DN_SUB, :]
            grow = gr_ref[:, sub * DN_SUB:(sub + 1) * DN_SUB]
            cum_c = _chunk_cumsum(gcol, 0, rev)
            cum_r = _chunk_cumsum(grow, 1, rev)
            if rev:
                incl = same & (row <= col)
                strict = same & (row < col)
            else:
                incl = same & (row >= col)
                strict = same & (row > col)
            for h in range(N_DN):
                ci = d * N_DN + h
                for u in range(ntile):
                    loc = slice(u * bt, (u + 1) * bt)
                    tok = slice(sub * DN_SUB + u * bt, sub * DN_SUB + (u + 1) * bt)
                    probs[d, h, u] = dict(
                        rev=rev, row0=blk * DN_BT + sub * DN_SUB + u * bt, incl=incl, strict=strict,
                        gc_c=cum_c[loc, ci:ci + 1], gc_r=cum_r[ci:ci + 1, loc],
                        b_c=gcol[loc, 2 * N_DN + ci:2 * N_DN + ci + 1],
                        q=qkv_ref[tok, h * LANES:(h + 1) * LANES],
                        k=qkv_ref[tok, DN_W + h * LANES:DN_W + (h + 1) * LANES],
                        v=qkv_ref[tok, 2 * DN_W + h * LANES:2 * DN_W + (h + 1) * LANES])
        plist = list(probs.values())

        for p in plist:
            gam = jnp.exp(jnp.where(p["incl"], p["gc_c"] - p["gc_r"], -jnp.inf))
            kk = _dot_nt(p["k"], p["k"])
            qk = _dot_nt(p["q"], p["k"])
            a = jnp.where(p["strict"], p["b_c"] * kk * gam, 0.0)
            p["ab"] = a.astype(BF16)
            p["qkg"] = (qk * gam).astype(BF16)
            p["db"] = (eye - jnp.where(level(1), a, 0.0)).astype(BF16)

        b = 2
        while b < CHUNK:
            lm = level(b)
            for p in plist:
                p["x"] = jnp.where(lm, _dot(p["db"], p["ab"]), 0.0).astype(BF16)
            for p in plist:
                p["db"] = jnp.where(lm, (-_dot(p["x"], p["db"])).astype(BF16), p["db"])
            b *= 2

        for p in plist:
            e_c = jnp.exp(p["gc_c"])
            kf = p["k"].astype(F32)
            vb = (p["v"].astype(F32) * p["b_c"]).astype(BF16)
            ke = (kf * (e_c * p["b_c"])).astype(BF16)
            qd = (p["q"].astype(F32) * e_c).astype(BF16)
            uw = _dot(p["db"], jnp.concatenate([vb, ke], axis=1))
            p["u"] = uw[:, 0:LANES]
            w = uw[:, LANES:2 * LANES].astype(BF16)
            p["wq"], p["kd"], p["egl"] = [], [], []
            for c in range(nchunk):
                rows = slice(c * CHUNK, (c + 1) * CHUNK)
                last = c * CHUNK if p["rev"] else (c + 1) * CHUNK - 1
                gl = p["gc_c"][last:last + 1, :]
                p["wq"].append(jnp.concatenate([w[rows], qd[rows]], axis=0))
                p["kd"].append((kf[rows] * jnp.exp(gl - p["gc_c"][rows])).astype(BF16))
                p["egl"].append(jnp.exp(gl))
            p["o1"] = [None] * nchunk

        states = {(d, h): s_ref[d, h] for d in range(2) for h in range(N_DN)}
        for t in range(ntile * nchunk):
            for (d, h), state in states.items():
                u, c = t // nchunk, t % nchunk
                if d == 1:
                    u, c = ntile - 1 - u, nchunk - 1 - c
                p = probs[d, h, u]
                i = (d * N_DN + h) * ntile + u
                rows = slice(c * CHUNK, (c + 1) * CHUNK)
                pq = _dot(p["wq"][c], state.astype(BF16))
                vnew = (p["u"][rows] - pq[0:CHUNK]).astype(BF16)
                p["o1"][c] = pq[CHUNK:2 * CHUNK]
                vn_ref[stage, i, rows, :] = vnew
                states[d, h] = state * p["egl"][c] + lax.dot_general(
                    p["kd"][c], vnew, (((0,), (0,)), ((), ())), preferred_element_type=F32)

        for (d, h), state in states.items():
            s_ref[d, h] = state
        for (d, h, u), p in probs.items():
            i = (d * N_DN + h) * ntile + u
            o = jnp.concatenate(p["o1"], axis=0) + _dot(p["qkg"], vn_ref[stage, i])
            rows = pl.ds(pl.multiple_of(p["row0"], bt), bt)
            o_ref[rows, h * LANES:(h + 1) * LANES] += o

    for stage in range(nstage):
        run_stage(stage)


def _deltanet(dnqkv, gcol, grow, B, S):
    T = dnqkv.shape[0]
    bt = DN_BT
    nb = S // bt
    fwd = lambda b, i: b * nb + i
    bwd = lambda b, i: b * nb + nb - 1 - i
    def specs(f):
        return [pl.BlockSpec((bt, 3 * DN_W), lambda b, i: (f(b, i), 0)),
                pl.BlockSpec((bt, 16), lambda b, i: (f(b, i), 0)),
                pl.BlockSpec((16, bt), lambda b, i: (0, f(b, i)))]
    return pl.pallas_call(
        _dn_kernel, grid=(B, nb),
        in_specs=specs(fwd) + specs(bwd),
        out_specs=pl.BlockSpec((S, DN_W), lambda b, i: (b, 0)),
        out_shape=jax.ShapeDtypeStruct((T, DN_W), F32),
        scratch_shapes=[pltpu.VMEM((2, N_DN, DK_DN, DK_DN), F32),
                        pltpu.VMEM((DN_BT // DN_SUB, 2 * N_DN * (DN_SUB // DN_TILE), DN_TILE, DK_DN), BF16)],
        compiler_params=_cparams(("parallel", "arbitrary")), name="deltanet",
    )(dnqkv, gcol, grow, dnqkv, gcol, grow)


ATTN_TKB = 1024


def _attn_kernel(S, tq, tk, lam_init,
                 lam_ref, q_ref, k_ref, v_ref, g_ref, o_ref, s_scr, vaug):
    @pl.when(pl.program_id(2) == 0)
    def _():
        vaug[:, 0:LANES] = v_ref[...]
        vaug[:, LANES:2 * LANES] = jnp.ones((S, LANES), BF16)

    lf = lam_ref[...]
    lam = (jnp.exp(jnp.sum(lf[0:1] * lf[1:2], axis=-1, keepdims=True))
           - jnp.exp(jnp.sum(lf[2:3] * lf[3:4], axis=-1, keepdims=True)) + lam_init)
    q = q_ref[...]
    lane = lax.broadcasted_iota(jnp.int32, q.shape, 1)
    zero = jnp.zeros_like(q)
    qs = (jnp.where(lane < DH_DA, q, zero), jnp.where(lane < DH_DA, zero, q))

    m = []
    for c in range(2):
        mp = jnp.full((tq, LANES), -jnp.inf, F32)
        for i in range(S // tk):
            s = _dot_nt(qs[c], k_ref[i * tk:(i + 1) * tk, :])
            s_scr[c, :, i * tk:(i + 1) * tk] = s
            for j in range(tk // LANES):
                mp = jnp.maximum(mp, s[:, j * LANES:(j + 1) * LANES])
        m.append(jnp.max(mp, axis=-1, keepdims=True))

    outs = []
    for c in range(2):
        acc = jnp.zeros((tq, 2 * LANES), F32)
        for i in range(S // ATTN_TKB):
            sl = slice(i * ATTN_TKB, (i + 1) * ATTN_TKB)
            p = jnp.exp2(s_scr[c, :, sl] - m[c]).astype(BF16)
            acc = acc + _dot(p, vaug[sl, :])
        outs.append(acc)
    o1 = outs[0][:, 0:LANES] / outs[0][:, LANES:2 * LANES]
    o2 = outs[1][:, 0:LANES] / outs[1][:, LANES:2 * LANES]
    o_ref[...] = (_rms(o1 - lam * o2, g_ref[...]) * (1.0 - lam_init)).astype(BF16)


def _attn(aq, ak, av, da_lambda, subln_g, B, S, lam_init, tq=1024, tk=512):
    T = aq.shape[0]
    nq = S // tq
    kern = functools.partial(_attn_kernel, S, tq, tk, lam_init)
    kv_spec = pl.BlockSpec((S, LANES), lambda b, h, i: (b, h))
    q_spec = pl.BlockSpec((tq, LANES), lambda b, h, i: (b * nq + i, h))
    return pl.pallas_call(
        kern, grid=(B, N_DA, nq),
        in_specs=[_const_spec(da_lambda.shape), q_spec, kv_spec, kv_spec,
                  _const_spec(subln_g.shape)],
        out_specs=q_spec,
        out_shape=jax.ShapeDtypeStruct((T, DN_W), BF16),
        scratch_shapes=[pltpu.VMEM((2, tq, S), F32), pltpu.VMEM((S, 2 * LANES), BF16)],
        compiler_params=_cparams(("parallel", "parallel", "arbitrary")), name="attn",
    )(da_lambda, aq, ak, av, subln_g)


FFN_FC = 256
FFN_NC = D_FF // FFN_FC


def _ffn_kernel(tiles_per_seq, tm, *refs):
    (xp_ref, x_ref, xn_ref, op_ref, o_ref, on_ref, zp_ref, z_ref, zn_ref,
     ap_ref, a_ref, an_ref, pe_ref,
     dng_ref, wout_ref, ln2_ref, wg_ref, wu_ref, cw_ref, cb_ref, wd_ref,
     pproj_ref, png_ref, pgn_ref, wpg_ref, out_ref, gbuf, ubuf, act_scr) = refs
    j = pl.program_id(0) % tiles_per_seq
    cat = lambda p, m, n: jnp.concatenate([p[...], m[...], n[...]], axis=0)

    o = cat(op_ref, o_ref, on_ref)
    z = cat(zp_ref, z_ref, zn_ref)
    dng = dng_ref[...]
    parts = []
    for h in range(N_DN):
        cols = slice(h * LANES, (h + 1) * LANES)
        parts.append((_rms(o[:, cols], dng) * _silu(z[:, cols].astype(F32))).astype(BF16))
    parts.append(cat(ap_ref, a_ref, an_ref))
    x1 = cat(xp_ref, x_ref, xn_ref) + _dot(jnp.concatenate(parts, axis=1), wout_ref[...])
    rowi = lax.broadcasted_iota(jnp.int32, (tm + 2 * HALO, 1), 0)
    pad = ((rowi < HALO) & (j == 0)) | ((rowi >= HALO + tm) & (j == tiles_per_seq - 1))
    x1 = jnp.where(pad, 0.0, x1)

    hn = _rms(x1, ln2_ref[...]).astype(BF16)
    hm = hn[HALO:HALO + tm]
    r = FFN_CONV // 2

    def matmuls(c, slot):
        gbuf[slot] = _dot(hn, wg_ref[c])
        ubuf[slot] = _dot(hm, wu_ref[c])

    def activate(c, slot):
        cw = cw_ref[c]
        gate = gbuf[slot, pl.ds(HALO - r, tm), :] * cw[0:1]
        for t in range(1, FFN_CONV):
            gate = gate + gbuf[slot, pl.ds(HALO - r + t, tm), :] * cw[t:t + 1]
        gate = gate + cb_ref[c]
        act = (_silu(gate) * ubuf[slot]).astype(BF16)
        act_scr[:, c * FFN_FC:(c + 1) * FFN_FC] = act

    matmuls(0, 0)
    for c in range(1, FFN_NC):
        matmuls(c, c % 2)
        activate(c - 1, (c - 1) % 2)
    activate(FFN_NC - 1, (FFN_NC - 1) % 2)
    x2 = x1[HALO:HALO + tm] + _dot(act_scr[...], wd_ref[...])
    e = _rms(_dot(pe_ref[...].astype(BF16), pproj_ref[...]), png_ref[...])
    gt = jax.nn.sigmoid(_dot(_rms(x2, pgn_ref[...]).astype(BF16), wpg_ref[...]))
    out_ref[...] = x2 + gt * e


def _ffn(x, osum, z, oda, pe, pe_row0, S, prep, tm=512):
    T = x.shape[0]
    hb = tm // HALO
    nhb = T // HALO
    tps = S // tm
    kern = functools.partial(_ffn_kernel, tps, tm)
    tile = lambda w: pl.BlockSpec((tm, w), lambda i: (i, 0))

    def with_halo(w):
        return [pl.BlockSpec((HALO, w), lambda i: (jnp.maximum(i * hb - 1, 0), 0)), tile(w),
                pl.BlockSpec((HALO, w), lambda i: (jnp.minimum((i + 1) * hb, nhb - 1), 0))]

    names = ("dn_norm_g", "w_out", "ln2_g", "w_gate", "w_upp", "ffn_conv_w", "ffn_conv_b", "w_down",
             "ple_proj", "ple_norm_g", "ple_gate_norm_g", "w_ple_gate")
    in_specs = (with_halo(D_MODEL) + with_halo(DN_W) + with_halo(DN_W) + with_halo(DN_W)
                + [pl.BlockSpec((tm, PLE_DIM), lambda i: (i + pe_row0 // tm, 0))]
                + [pl.BlockSpec(prep[n].shape, functools.partial(lambda nd, i: (0,) * nd, prep[n].ndim),
                                pipeline_mode=pl.Buffered(1)) for n in names])
    return pl.pallas_call(
        kern, grid=(T // tm,), in_specs=in_specs, out_specs=tile(D_MODEL),
        out_shape=jax.ShapeDtypeStruct((T, D_MODEL), F32),
        scratch_shapes=[pltpu.VMEM((2, tm + 2 * HALO, FFN_FC), F32),
                        pltpu.VMEM((2, tm, FFN_FC), F32),
                        pltpu.VMEM((tm, D_FF), BF16)],
        compiler_params=_cparams(("parallel",)), name="ffn",
    )(x, x, x, osum, osum, osum, z, z, z, oda, oda, oda, pe, *[prep[n] for n in names])


def _rope_tables(S):
    half = ROT_DIM // 2
    inv_freq = ROPE_THETA ** (-jnp.arange(0, ROT_DIM, 2, dtype=F32) / ROT_DIM)
    ang = jnp.arange(S, dtype=F32)[:, None] * inv_freq[None, :]
    cos, sin = jnp.cos(ang), jnp.sin(ang)
    zeros = jnp.zeros((S, DH_DA - ROT_DIM), F32)
    zh = jnp.zeros((S, half), F32)
    c = jnp.concatenate([cos, cos, zeros + 1.0], axis=1)
    s1 = jnp.concatenate([-sin, zh, zeros], axis=1)
    s2 = jnp.concatenate([zh, sin, zeros], axis=1)
    rep = lambda a: jnp.concatenate([a, a], axis=1)
    return rep(c), rep(s1), rep(s2)


def _prepare(layer, seqs, ln1_g, w_in, dn_conv_w, dn_a_log, dn_dt_bias, dn_norm_g, da_qk_norm_g,
             da_lambda, da_subln_g, w_out, ln2_g, w_up, ffn_conv_w, ffn_conv_b, w_down,
             ple_proj, ple_norm_g, ple_gate_norm_g, w_ple_gate):
    row = lambda a: a[layer].reshape(1, -1).astype(F32)
    w = w_in[layer]
    o_z, o_ab, o_aq = 3 * DN_W, 4 * DN_W, 4 * DN_W + 16
    w_rest = jnp.concatenate(
        [w[:, o_z:o_ab], w[:, o_aq:o_aq + 3 * DN_W], w[:, o_ab:o_aq],
         jnp.zeros((D_MODEL, LANES - 16), w.dtype)], axis=1)
    n8 = 2 * N_DN
    flat8 = lambda a: a[layer].reshape(n8).astype(F32)
    both = lambda a: jnp.zeros((n8, 2 * LANES), F32).at[0, 0:n8].set(flat8(a)).at[:, LANES].set(flat8(a))
    chunked = lambda a: a.reshape(a.shape[0], FFN_NC, FFN_FC).transpose(1, 0, 2)
    wu = w_up[layer]
    return {
        "ln1_g": row(ln1_g),
        "w_dn": w[:, 0:3 * DN_W].astype(BF16),
        "w_rest": w_rest.astype(BF16),
        "dn_conv_w": dn_conv_w[layer].astype(F32),
        "alog": both(dn_a_log), "dtb": both(dn_dt_bias),
        "qkg": jnp.tile(da_qk_norm_g[layer].astype(F32), (1, 2)),
        "rope": {S: _rope_tables(S) for S in seqs},
        "dn_norm_g": row(dn_norm_g),
        "da_lambda": da_lambda[layer].astype(F32),
        "da_subln_g": row(da_subln_g),
        "w_out": w_out[layer].astype(BF16),
        "ln2_g": row(ln2_g),
        "w_gate": chunked(wu[:, 0:D_FF]).astype(BF16),
        "w_upp": chunked(wu[:, D_FF:]).astype(BF16),
        "ffn_conv_w": chunked(ffn_conv_w[layer].astype(F32)),
        "ffn_conv_b": ffn_conv_b[layer].astype(F32).reshape(FFN_NC, 1, FFN_FC),
        "w_down": w_down[layer].astype(BF16),
        "ple_proj": ple_proj[layer].astype(BF16),
        "ple_norm_g": row(ple_norm_g),
        "ple_gate_norm_g": row(ple_gate_norm_g),
        "w_ple_gate": w_ple_gate[layer].astype(BF16),
    }


def _encoder_layer(x, pe_all, layer, prep):
    B, S, _ = x.shape
    lam_init = 0.8 - 0.6 * math.exp(-0.3 * layer)
    xf = x.reshape(B * S, D_MODEL)
    dnqkv, z, gcol, grow, aq, ak, av = _inproj(xf, S, prep)
    osum = _deltanet(dnqkv, gcol, grow, B, S)
    oda = _attn(aq, ak, av, prep["da_lambda"], prep["da_subln_g"], B, S, lam_init)
    out = _ffn(xf, osum, z, oda, pe_all.reshape(-1, PLE_DIM), layer * B * S, S, prep)
    return out.reshape(B, S, D_MODEL)


def kernel(x_prompt, x_sample, p_prompt, p_sample, ln1_g, w_in, dn_conv_w, dn_a_log, dn_dt_bias, dn_norm_g, da_qk_norm_g, da_lambda, da_subln_g, w_out, ln2_g, w_up, ffn_conv_w, ffn_conv_b, w_down, ple_proj, ple_norm_g, ple_gate_norm_g, w_ple_gate):
    weights = (ln1_g, w_in, dn_conv_w, dn_a_log, dn_dt_bias, dn_norm_g, da_qk_norm_g, da_lambda,
               da_subln_g, w_out, ln2_g, w_up, ffn_conv_w, ffn_conv_b, w_down, ple_proj,
               ple_norm_g, ple_gate_norm_g, w_ple_gate)
    depth = w_in.shape[0]
    seqs = (x_prompt.shape[1], x_sample.shape[1])
    outs = []
    for x, p in ((x_prompt, p_prompt), (x_sample, p_sample)):
        for layer in range(depth):
            prep = _prepare(layer, seqs, *weights)
            x = _encoder_layer(x, p, layer, prep)
        outs.append(x)
    return tuple(outs)
```
